```python
import jax, jax.numpy as jnp
from jax import lax
import numpy as np

D_MODEL = 2048
BATCH = 16
SEQ = 2048
DEPTH = 1
DEC_BATCH = 32
DEC_SEQ = 4
PAST_LEN = 16384
PAGE_SIZE = 128

MLSTM_HEADS = 4
MLSTM_QK = 128
MLSTM_V = 256
MLSTM_CHUNK = 64
ATTN_HEADS = 8
ATTN_HD = 128
IDX_HEADS = 16
IDX_DIM = 64
TOPK_MAX = 256
Q_BLOCK = 128
MEM_TOKENS = 256
MEM_HEADS = 4
MEM_HD = D_MODEL // MEM_HEADS
MIX_WIDTH = MLSTM_HEADS * MLSTM_V + ATTN_HEADS * ATTN_HD
FFN_HIDDEN = ((8 * D_MODEL + 2) // 3 + 255) // 256 * 256
ROPE_THETA = 10000.0
LN_EPS = 1e-5
DEEPNORM_ALPHA = (2 * DEPTH) ** 0.25
DEEPNORM_BETA = (8 * DEPTH) ** -0.25
IN_SPLITS = (MLSTM_HEADS * MLSTM_QK, MLSTM_HEADS * MLSTM_QK, MLSTM_HEADS * MLSTM_V, MLSTM_HEADS * MLSTM_V,
             MLSTM_HEADS, MLSTM_HEADS,
             ATTN_HEADS * ATTN_HD, ATTN_HEADS * ATTN_HD, ATTN_HEADS * ATTN_HD,
             IDX_HEADS * IDX_DIM, IDX_DIM, IDX_HEADS)
IN_COLS = sum(IN_SPLITS)

kernel_name = 'hymba_mlstm_dsa_deepnorm_step'


def split_columns(z):
    parts, start = [], 0
    for size in IN_SPLITS:
        parts.append(z[..., start:start + size])
        start += size
    return parts


def layer_norm(x, g, b):
    xf = x.astype(jnp.float32)
    mu = xf.mean(-1, keepdims=True)
    var = jnp.square(xf - mu).mean(-1, keepdims=True)
    return ((xf - mu) * lax.rsqrt(var + LN_EPS) * g + b).astype(x.dtype)


def rope(x, pos):
    half = x.shape[-1] // 2
    inv_freq = ROPE_THETA ** (-jnp.arange(half, dtype=jnp.float32) / half)
    ang = pos.astype(jnp.float32)[:, None] * inv_freq
    cos, sin = jnp.cos(ang)[:, None, :], jnp.sin(ang)[:, None, :]
    x1, x2 = x[..., :half].astype(jnp.float32), x[..., half:].astype(jnp.float32)
    return jnp.concatenate([x1 * cos - x2 * sin, x2 * cos + x1 * sin], axis=-1).astype(x.dtype)


def mixer_inputs(x, w_in, b_ig, b_fg, pos):
    B, T, _ = x.shape
    mq, mk, mv, mo, ig, fg, aq, ak, av, iq, ik, iw = split_columns(x @ w_in)
    mlstm_in = (mq.reshape(B, T, MLSTM_HEADS, MLSTM_QK), mk.reshape(B, T, MLSTM_HEADS, MLSTM_QK),
                mv.reshape(B, T, MLSTM_HEADS, MLSTM_V), mo, ig + b_ig, fg + b_fg)
    attn_in = (rope(aq.reshape(B, T, ATTN_HEADS, ATTN_HD), pos), rope(ak.reshape(B, T, ATTN_HEADS, ATTN_HD), pos),
               av.reshape(B, T, ATTN_HEADS, ATTN_HD), rope(iq.reshape(B, T, IDX_HEADS, IDX_DIM), pos),
               rope(ik[:, :, None, :], pos)[:, :, 0], iw)
    return mlstm_in, attn_in


def mlstm(q, k, v, ig, fg, C0, n0, m0):
    B, T, H, dk = q.shape
    dv = v.shape[-1]
    L = MLSTM_CHUNK if T % MLSTM_CHUNK == 0 else T
    nc = T // L

    def chunks(a):
        return jnp.moveaxis(a.astype(jnp.float32).reshape(B, nc, L, *a.shape[2:]), (1, 2), (0, 3))

    qc, kc, vc = chunks(q), chunks(k) * dk ** -0.5, chunks(v)
    ic, lfc = chunks(ig), jax.nn.log_sigmoid(chunks(fg))
    causal = jnp.tril(jnp.ones((L, L), dtype=bool))

    def step(carry, inp):
        C, n, m = carry
        q_, k_, v_, i_, lf = inp
        b = jnp.cumsum(lf, axis=-1)
        dlog = jnp.where(causal, b[..., :, None] - b[..., None, :] + i_[..., None, :], -jnp.inf)
        inter = b + m[..., None]
        m_t = jnp.maximum(inter, dlog.max(-1))
        w_prev = jnp.exp(inter - m_t)
        s = jnp.einsum('bhtk,bhsk->bhts', q_, k_) * jnp.exp(dlog - m_t[..., None])
        num = w_prev[..., None] * jnp.einsum('bhvk,bhtk->bhtv', C, q_) + jnp.einsum('bhts,bhsv->bhtv', s, v_)
        nq = w_prev * jnp.einsum('bhk,bhtk->bht', n, q_) + s.sum(-1)
        h = num / jnp.maximum(jnp.abs(nq), jnp.exp(-m_t))[..., None]
        m_new = m_t[..., -1]
        decay = jnp.exp(b[..., -1] + m - m_new)
        ws = jnp.exp(b[..., -1:] - b + i_ - m_new[..., None])
        C_new = decay[..., None, None] * C + jnp.einsum('bhs,bhsv,bhsk->bhvk', ws, v_, k_)
        n_new = decay[..., None] * n + jnp.einsum('bhs,bhsk->bhk', ws, k_)
        return (C_new, n_new, m_new), h

    carry0 = (C0.astype(jnp.float32), n0.astype(jnp.float32), m0.astype(jnp.float32))
    (C, n, m), h = lax.scan(step, carry0, (qc, kc, vc, ic, lfc))
    h = jnp.moveaxis(h, (0, 3), (1, 2)).reshape(B, T, H, dv)
    return h, C, n, m


def index_scores(qi, w, ki):
    dots = jax.nn.relu(jnp.einsum('...thd,...sd->...ths', qi, ki).astype(jnp.float32))
    return jnp.einsum('...th,...ths->...ts', w.astype(jnp.float32), dots) * (IDX_HEADS * IDX_DIM) ** -0.5


def attend_selected(q, k_sel, v_sel, valid):
    logits = jnp.einsum('...thd,...tkhd->...thk', q, k_sel).astype(jnp.float32) * q.shape[-1] ** -0.5
    logits = jnp.where(valid[..., None, :], logits, -jnp.inf)
    p = jax.nn.softmax(logits, axis=-1).astype(v_sel.dtype)
    return jnp.einsum('...thk,...tkhd->...thd', p, v_sel)


def dsa_prompt(q, k, v, qi, ki, w, n_sel):
    B, T, H, d = q.shape
    qb = min(Q_BLOCK, T)
    key_pos = jnp.arange(T)

    def one_seq(args):
        q_s, k_s, v_s, qi_s, ki_s, w_s = args

        def one_block(start):
            q_b = lax.dynamic_slice_in_dim(q_s, start, qb, 0)
            qi_b = lax.dynamic_slice_in_dim(qi_s, start, qb, 0)
            w_b = lax.dynamic_slice_in_dim(w_s, start, qb, 0)
            q_pos = start + jnp.arange(qb)
            scores = jnp.where(key_pos[None, :] <= q_pos[:, None], index_scores(qi_b, w_b, ki_s), -jnp.inf)
            vals, idx = lax.top_k(scores, n_sel)
            return attend_selected(q_b, k_s[idx], v_s[idx], jnp.isfinite(vals))

        return lax.map(one_block, jnp.arange(0, T, qb)).reshape(T, H, d)

    return lax.map(one_seq, (q, k, v, qi, ki, w))


def dsa_sample(layer, q, k_new, v_new, qi, ki_new, w, cache_k, cache_v, cache_idx_k, page_table, n_sel):
    DB, TN = q.shape[:2]
    past = page_table.shape[1] * PAGE_SIZE
    ki_past = cache_idx_k[layer, page_table].reshape(DB, past, IDX_DIM)
    ki_all = jnp.concatenate([ki_past, ki_new.astype(ki_past.dtype)], axis=1)
    key_pos = jnp.arange(past + TN)
    q_pos = past + jnp.arange(TN)
    scores = jnp.where(key_pos[None, None, :] <= q_pos[None, :, None], index_scores(qi, w, ki_all), -jnp.inf)
    vals, idx = lax.top_k(scores, n_sel)
    in_past = idx < past
    p_idx = jnp.minimum(idx, past - 1)
    phys = jnp.take_along_axis(page_table, (p_idx // PAGE_SIZE).reshape(DB, -1), axis=1).reshape(idx.shape)
    off = p_idx % PAGE_SIZE
    n_idx = jnp.clip(idx - past, 0, TN - 1)
    take_rows = jax.vmap(lambda rows, i: rows[i])
    k_sel = jnp.where(in_past[..., None, None], cache_k[layer, phys, off], take_rows(k_new, n_idx))
    v_sel = jnp.where(in_past[..., None, None], cache_v[layer, phys, off], take_rows(v_new, n_idx))
    return attend_selected(q, k_sel, v_sel, jnp.isfinite(vals))


def mixer_output(h_m, o_gate, norm_g, h_a, w_out):
    B, T = h_a.shape[:2]
    mu = h_m.mean(-1, keepdims=True)
    var = jnp.square(h_m - mu).mean(-1, keepdims=True)
    h_m = ((h_m - mu) * lax.rsqrt(var + LN_EPS)).reshape(B, T, -1) * norm_g * jax.nn.sigmoid(o_gate.astype(jnp.float32))
    merged = jnp.concatenate([h_m.astype(h_a.dtype), h_a.reshape(B, T, -1)], axis=-1)
    return merged @ w_out


def cross_attention(x, mem_k, mem_v, w_cq, w_co):
    B, T, _ = x.shape
    q = (x @ w_cq).reshape(B, T, MEM_HEADS, MEM_HD)
    logits = jnp.einsum('bthd,bshd->bths', q, mem_k).astype(jnp.float32) * MEM_HD ** -0.5
    p = jax.nn.softmax(logits, axis=-1).astype(mem_v.dtype)
    return jnp.einsum('bths,bshd->bthd', p, mem_v).reshape(B, T, D_MODEL) @ w_co


def swiglu(x, w_gate, w_up, w_down):
    return (jax.nn.silu(x @ w_gate) * (x @ w_up)) @ w_down


def block_tail(x, mix, mem_k, mem_v, ln1_g, ln1_b, w_cq, w_co, ln2_g, ln2_b, w_gate, w_up, w_down, ln3_g, ln3_b):
    x = layer_norm(DEEPNORM_ALPHA * x + mix, ln1_g, ln1_b)
    x = layer_norm(DEEPNORM_ALPHA * x + cross_attention(x, mem_k, mem_v, w_cq, w_co), ln2_g, ln2_b)
    return layer_norm(DEEPNORM_ALPHA * x + swiglu(x, w_gate, w_up, w_down), ln3_g, ln3_b)


def setup_inputs(seed: int = 0) -> dict:
    key = jax.random.key(seed)
    ks = jax.random.split(key, 32)
    f32 = jnp.float32
    n_pages = PAST_LEN // PAGE_SIZE
    n_used = DEC_BATCH * n_pages
    n_pool = n_used + n_used // 4

    def nrm(k, shape, scale=1.0):
        return jax.random.normal(k, shape, f32) * scale

    return {
        'x_prompt': nrm(ks[0], (BATCH, SEQ, D_MODEL)),
        'x_sample': nrm(ks[1], (DEC_BATCH, DEC_SEQ, D_MODEL)),
        'cache_k': nrm(ks[2], (DEPTH, n_pool, PAGE_SIZE, ATTN_HEADS, ATTN_HD)),
        'cache_v': nrm(ks[3], (DEPTH, n_pool, PAGE_SIZE, ATTN_HEADS, ATTN_HD)),
        'cache_idx_k': nrm(ks[4], (DEPTH, n_pool, PAGE_SIZE, IDX_DIM)),
        'state_C': nrm(ks[5], (DEPTH, DEC_BATCH, MLSTM_HEADS, MLSTM_V, MLSTM_QK), 0.5),
        'state_n': nrm(ks[6], (DEPTH, DEC_BATCH, MLSTM_HEADS, MLSTM_QK), 0.5),
        'state_m': nrm(ks[7], (DEPTH, DEC_BATCH, MLSTM_HEADS), 0.5),
        'cache_mem_k': nrm(ks[8], (DEPTH, DEC_BATCH, MEM_TOKENS, MEM_HEADS, MEM_HD)),
        'cache_mem_v': nrm(ks[9], (DEPTH, DEC_BATCH, MEM_TOKENS, MEM_HEADS, MEM_HD)),
        'page_table': jax.random.permutation(ks[10], n_pool)[:n_used].reshape(DEC_BATCH, n_pages).astype(jnp.int32),
        'mem_prompt': nrm(ks[11], (BATCH, MEM_TOKENS, D_MODEL)),
        'w_in': nrm(ks[12], (DEPTH, D_MODEL, IN_COLS), D_MODEL ** -0.5),
        'b_igate': nrm(ks[13], (DEPTH, MLSTM_HEADS), 0.5),
        'b_fgate': 3.0 + nrm(ks[14], (DEPTH, MLSTM_HEADS), 0.5),
        'mlstm_norm_g': 1.0 + nrm(ks[15], (DEPTH, MLSTM_HEADS * MLSTM_V), 0.02),
        'w_out': nrm(ks[16], (DEPTH, MIX_WIDTH, D_MODEL), MIX_WIDTH ** -0.5 * DEEPNORM_BETA),
        'ln1_g': 1.0 + nrm(ks[17], (DEPTH, D_MODEL), 0.02),
        'ln1_b': nrm(ks[18], (DEPTH, D_MODEL), 0.02),
        'w_cq': nrm(ks[19], (DEPTH, D_MODEL, D_MODEL), D_MODEL ** -0.5),
        'w_ck': nrm(ks[20], (DEPTH, D_MODEL, D_MODEL), D_MODEL ** -0.5),
        'w_cv': nrm(ks[21], (DEPTH, D_MODEL, D_MODEL), D_MODEL ** -0.5),
        'w_co': nrm(ks[22], (DEPTH, D_MODEL, D_MODEL), D_MODEL ** -0.5 * DEEPNORM_BETA),
        'ln2_g': 1.0 + nrm(ks[23], (DEPTH, D_MODEL), 0.02),
        'ln2_b': nrm(ks[24], (DEPTH, D_MODEL), 0.02),
        'w_gate': nrm(ks[25], (DEPTH, D_MODEL, FFN_HIDDEN), D_MODEL ** -0.5),
        'w_up': nrm(ks[26], (DEPTH, D_MODEL, FFN_HIDDEN), D_MODEL ** -0.5),
        'w_down': nrm(ks[27], (DEPTH, FFN_HIDDEN, D_MODEL), FFN_HIDDEN ** -0.5 * DEEPNORM_BETA),
        'ln3_g': 1.0 + nrm(ks[28], (DEPTH, D_MODEL), 0.02),
        'ln3_b': nrm(ks[29], (DEPTH, D_MODEL), 0.02),
    }


def reference(x_prompt, x_sample, cache_k, cache_v, cache_idx_k, state_C, state_n, state_m, cache_mem_k, cache_mem_v,
              page_table, mem_prompt, w_in, b_igate, b_fgate, mlstm_norm_g, w_out, ln1_g, ln1_b, w_cq, w_ck, w_cv, w_co,
              ln2_g, ln2_b, w_gate, w_up, w_down, ln3_g, ln3_b):
    B, T = x_prompt.shape[:2]
    DB, TN = x_sample.shape[:2]
    past = page_table.shape[1] * PAGE_SIZE
    pos_p = jnp.arange(T)
    pos_s = past + jnp.arange(TN)
    n_sel_p = min(TOPK_MAX, T // 4)
    n_sel_s = min(TOPK_MAX, (past + TN) // 4)
    n_mem = mem_prompt.shape[1]
    xp, xs = x_prompt, x_sample
    new = [[] for _ in range(14)]
    for l in range(DEPTH):
        tail_w = (ln1_g[l], ln1_b[l], w_cq[l], w_co[l], ln2_g[l], ln2_b[l], w_gate[l], w_up[l], w_down[l], ln3_g[l], ln3_b[l])
        (mq, mk, mv, mo, ig, fg), (aq, ak, av, iq, ik, iw) = mixer_inputs(xp, w_in[l], b_igate[l], b_fgate[l], pos_p)
        C0 = jnp.zeros((B, MLSTM_HEADS, MLSTM_V, MLSTM_QK), jnp.float32)
        n0 = jnp.zeros((B, MLSTM_HEADS, MLSTM_QK), jnp.float32)
        m0 = jnp.zeros((B, MLSTM_HEADS), jnp.float32)
        h_m, pC, pn, pm = mlstm(mq, mk, mv, ig, fg, C0, n0, m0)
        h_a = dsa_prompt(aq, ak, av, iq, ik, iw, n_sel_p)
        mem_k = (mem_prompt @ w_ck[l]).reshape(B, n_mem, MEM_HEADS, MEM_HD)
        mem_v = (mem_prompt @ w_cv[l]).reshape(B, n_mem, MEM_HEADS, MEM_HD)
        xp = block_tail(xp, mixer_output(h_m, mo, mlstm_norm_g[l], h_a, w_out[l]), mem_k, mem_v, *tail_w)
        (sq, sk, sv, so, sig, sfg), (saq, sak, sav, siq, sik, siw) = mixer_inputs(xs, w_in[l], b_igate[l], b_fgate[l], pos_s)
        sh_m, sC, sn, sm = mlstm(sq, sk, sv, sig, sfg, state_C[l], state_n[l], state_m[l])
        sh_a = dsa_sample(l, saq, sak, sav, siq, sik, siw, cache_k, cache_v, cache_idx_k, page_table, n_sel_s)
        xs = block_tail(xs, mixer_output(sh_m, so, mlstm_norm_g[l], sh_a, w_out[l]), cache_mem_k[l], cache_mem_v[l], *tail_w)
        for slot, val in zip(new, (ak, av, ik, pC, pn, pm, mem_k, mem_v, sak, sav, sik, sC, sn, sm)):
            slot.append(val)
    (p_k, p_v, p_idx_k, p_C, p_n, p_m, p_mem_k, p_mem_v,
     s_k, s_v, s_idx_k, s_C, s_n, s_m) = [jnp.stack(slot) for slot in new]
    return (xp, xs, p_k, p_v, p_idx_k, p_C, p_n, p_m, p_mem_k, p_mem_v, s_k, s_v, s_idx_k, s_C, s_n, s_m)
```

```python
import functools

import jax
import jax.numpy as jnp
from jax import lax
from jax.experimental import pallas as pl
from jax.experimental.pallas import tpu as pltpu

MLSTM_HEADS = 4
MLSTM_QK = 128
MLSTM_V = 256
ATTN_HEADS = 8
ATTN_HD = 128
IDX_HEADS = 16
IDX_DIM = 64
TOPK_MAX = 256
PAGE_SIZE = 128
MEM_HEADS = 4
ROPE_THETA = 10000.0
LN_EPS = 1e-5

LANES = 128
SUBLANES = 8
VMEM_LIMIT_BYTES = 52 * 1024 * 1024

SM_IK = 0
SM_IG = IDX_DIM
SM_FG = IDX_DIM + MLSTM_HEADS
SM_IW = IDX_DIM + 2 * MLSTM_HEADS

MLSTM_CHUNK = 256
DSA_QB = 256
MLSTM_EXT = MLSTM_V + LANES
NEG_BIG = -1e30
INT_MIN = -(2 ** 31)

F32 = jnp.float32
BF16 = jnp.bfloat16


def _cparams(sem):
    return pltpu.CompilerParams(dimension_semantics=sem, vmem_limit_bytes=VMEM_LIMIT_BYTES)


def _dot(a, b):
    return jnp.dot(a, b, preferred_element_type=F32)


def _dot_nt(a, b):
    return lax.dot_general(a, b, (((1,), (1,)), ((), ())), preferred_element_type=F32)


def _dot_tn(a, b):
    return lax.dot_general(a, b, (((0,), (0,)), ((), ())), preferred_element_type=F32)


def _split3(x):
    p0 = x.astype(BF16)
    r = x - p0.astype(F32)
    p1 = r.astype(BF16)
    r = r - p1.astype(F32)
    return p0, p1, r.astype(BF16)


def _dot_exact_rhs(a_b16, x_f32):
    p0, p1, p2 = _split3(x_f32)
    return _dot(a_b16, p0) + _dot(a_b16, p1) + _dot(a_b16, p2)


def _proj_kernel(*refs, rope, has_bias, n_out, tn):
    x_ref, w_ref = refs[0], refs[1]
    pos = 2
    cos_ref = sin_ref = bias_ref = None
    if rope:
        cos_ref, sin_ref = refs[pos], refs[pos + 1]
        pos += 2
    if has_bias:
        bias_ref = refs[pos]
        pos += 1
    out_refs = refs[pos:pos + n_out]
    acc = _dot(x_ref[...], w_ref[...])
    tm = acc.shape[0]
    if rope:
        cos = cos_ref[...]
        sin = sin_ref[...]
        lane = lax.broadcasted_iota(jnp.int32, (tm, LANES), 1)
    for j in range(tn // LANES):
        blk = acc[:, j * LANES:(j + 1) * LANES]
        if rope == 1:
            partner = pltpu.roll(blk, 64, 1)
            blk = blk * cos + partner * sin
        elif rope == 2:
            partner = jnp.where((lane % 64) < 32, pltpu.roll(blk, 96, 1), pltpu.roll(blk, 32, 1))
            blk = blk * cos + partner * sin
        if has_bias:
            blk = blk + bias_ref[...]
        for o in out_refs:
            o[:, j * LANES:(j + 1) * LANES] = blk.astype(o.dtype)


def _proj(x, w, out_dtypes, rope=0, cos=None, sin=None, bias=None, name="proj"):
    M, K = x.shape
    N = w.shape[1]
    tm = min(1024, M)
    tn = min(1024, N)
    assert M % tm == 0 and N % tn == 0 and tn % LANES == 0
    grid = (N // tn, M // tm)
    in_specs = [pl.BlockSpec((tm, K), lambda n, m: (m, 0)),
                pl.BlockSpec((K, tn), lambda n, m: (0, n))]
    args = [x, w]
    if rope:
        if cos.shape[0] < tm:
            assert tm % cos.shape[0] == 0
            cos = jnp.tile(cos, (tm // cos.shape[0], 1))
            sin = jnp.tile(sin, (tm // sin.shape[0], 1))
        rows = cos.shape[0]
        assert rows % tm == 0
        nrb = rows // tm
        tab_spec = pl.BlockSpec((tm, LANES), lambda n, m: (m % nrb, 0))
        in_specs += [tab_spec, tab_spec]
        args += [cos, sin]
    if bias is not None:
        assert tn == LANES
        in_specs.append(pl.BlockSpec((1, LANES), lambda n, m: (0, 0)))
        args.append(bias)
    out_shape = [jax.ShapeDtypeStruct((M, N), dt) for dt in out_dtypes]
    out_specs = [pl.BlockSpec((tm, tn), lambda n, m: (m, n)) for _ in out_dtypes]
    outs = pl.pallas_call(
        functools.partial(_proj_kernel, rope=rope, has_bias=bias is not None,
                          n_out=len(out_dtypes), tn=tn),
        out_shape=out_shape, grid=grid, in_specs=in_specs, out_specs=out_specs,
        compiler_params=_cparams(("parallel", "parallel")), name=name,
    )(*args)
    return outs


def _rope_tables(pos, head_dim):
    half = head_dim // 2
    inv_freq = ROPE_THETA ** (-jnp.arange(half, dtype=F32) / half)
    ang = pos.astype(F32)[:, None] * inv_freq
    cos, sin = jnp.cos(ang), jnp.sin(ang)
    reps = LANES // head_dim
    cos_full = jnp.tile(jnp.concatenate([cos, cos], axis=-1), (1, reps))
    sin_full = jnp.tile(jnp.concatenate([-sin, sin], axis=-1), (1, reps))
    return cos_full, sin_full


def _small_tables(pos):
    cos_i, sin_i = _rope_tables(pos, IDX_DIM)
    lane = jnp.arange(LANES)[None, :]
    return jnp.where(lane < IDX_DIM, cos_i, 1.0), jnp.where(lane < IDX_DIM, sin_i, 0.0)


def _mlstm_kernel(qkv_ref, g_ref, mo_ref, ng_ref, c0_ref, n0_ref, m0_ref, mrg_in_ref,
                  hm_ref, cout_ref, nout_ref, mout_ref, cext_ref, m_ref, *, L, valid):
    del mrg_in_ref
    H, DK, DV = MLSTM_HEADS, MLSTM_QK, MLSTM_V
    c = pl.program_id(1)
    nc = pl.num_programs(1)
    row8 = lax.broadcasted_iota(jnp.int32, (SUBLANES, DK), 0)

    @pl.when(c == 0)
    def _init():
        for h in range(H):
            cext_ref[h, 0:DV, :] = c0_ref[0, h]
            n_row = jnp.broadcast_to(n0_ref[0, h:h + 1, :], (SUBLANES, DK))
            cext_ref[h, DV:DV + SUBLANES, :] = jnp.where(row8 == 0, n_row, 0.0)
            cext_ref[h, DV + SUBLANES:, :] = jnp.zeros((MLSTM_EXT - DV - SUBLANES, DK), F32)
            m_ref[h] = jnp.broadcast_to(m0_ref[0, h:h + 1, :], (SUBLANES, LANES))

    scale = DK ** -0.5
    g = g_ref[...]
    row = lax.broadcasted_iota(jnp.int32, (L, 1), 0)
    ri = lax.broadcasted_iota(jnp.int32, (L, L), 0)
    ci = lax.broadcasted_iota(jnp.int32, (L, L), 1)
    causal = ci <= ri
    lt_b = jnp.where(causal, 1.0, 0.0).astype(BF16)
    ones_b = jnp.ones((L, L), BF16)
    lane_e = lax.broadcasted_iota(jnp.int32, (L, LANES), 1)
    one_col = jnp.where(lane_e == 0, 1.0, 0.0).astype(BF16)

    for h in range(H):
        q = qkv_ref[:, h * DK:(h + 1) * DK]
        k = qkv_ref[:, H * DK + h * DK:H * DK + (h + 1) * DK]
        v = qkv_ref[:, 2 * H * DK + h * DV:2 * H * DK + (h + 1) * DV]
        i_col = g[:, SM_IG + h:SM_IG + h + 1]
        lf_col = jax.nn.log_sigmoid(g[:, SM_FG + h:SM_FG + h + 1])
        if valid < L:
            i_col = jnp.where(row < valid, i_col, NEG_BIG)
            lf_col = jnp.where(row < valid, lf_col, 0.0)
        rhs = jnp.concatenate([jnp.where(ri > ci, lf_col, 0.0),
                               jnp.broadcast_to(lf_col, (L, LANES))], axis=1)
        cs = _dot_exact_rhs(lt_b, rhs)
        dsum = cs[:, :L]
        bcol = cs[:, L:L + 1]
        imat = _dot_exact_rhs(ones_b, jnp.where(ri == ci, i_col, 0.0))
        m_prev = m_ref[h, 0:1, 0:1]
        inter = bcol + m_prev
        dlog = jnp.where(causal, dsum + imat, -jnp.inf)
        m_t = jnp.maximum(inter, jnp.max(dlog, axis=1, keepdims=True))
        w_prev = jnp.exp(inter - m_t)
        s_mat = (_dot_nt(q, k) * scale) * jnp.exp(dlog - m_t)
        vext = jnp.concatenate([v, one_col], axis=1)
        cb = cext_ref[h].astype(BF16)
        numext = w_prev * _dot_nt(q, cb) + _dot(s_mat.astype(BF16), vext)
        num = numext[:, :DV]
        nq = numext[:, DV:DV + 1]
        hh = num / jnp.maximum(jnp.abs(nq), jnp.exp(-m_t))
        mu = jnp.mean(hh, axis=1, keepdims=True)
        var = jnp.mean(jnp.square(hh - mu), axis=1, keepdims=True)
        hn = (hh - mu) * lax.rsqrt(var + LN_EPS)
        gate = jax.nn.sigmoid(mo_ref[:, h * DV:(h + 1) * DV])
        hm_ref[:, h * DV:(h + 1) * DV] = (hn * ng_ref[:, h * DV:(h + 1) * DV] * gate).astype(hm_ref.dtype)
        m_new = m_t[L - 1:L, :]
        b_last = bcol[L - 1:L, :]
        decay = jnp.exp(b_last + m_prev - m_new)
        ws_col = jnp.exp(b_last - bcol + i_col - m_new) * scale
        xw = (ws_col * vext.astype(F32)).astype(BF16)
        cext_ref[h] = decay * cext_ref[h] + _dot_tn(xw, k)
        m_ref[h] = jnp.broadcast_to(m_new, (SUBLANES, LANES))

    @pl.when(c == nc - 1)
    def _fin():
        for h in range(H):
            cout_ref[0, h] = cext_ref[h, 0:DV, :]
            nout_ref[0, h:h + 1, :] = cext_ref[h, DV:DV + 1, :]
            mout_ref[0, h:h + 1, :] = m_ref[h, 0:1, :]


def _mlstm(qkv, small, mo, norm_g, c0, n0, m0, merged, *, nb, T, L, valid):
    H, DK, DV = MLSTM_HEADS, MLSTM_QK, MLSTM_V
    nc = T // L
    hw = H * DV
    m0r = jnp.broadcast_to(m0[:, :, None], (nb, H, LANES)).astype(F32)
    row_map = lambda b, c: (b * nc + c, 0)
    st4 = lambda b, c: (b, 0, 0, 0)
    st3 = lambda b, c: (b, 0, 0)
    outs = pl.pallas_call(
        functools.partial(_mlstm_kernel, L=L, valid=valid),
        out_shape=[jax.ShapeDtypeStruct(merged.shape, merged.dtype),
                   jax.ShapeDtypeStruct((nb, H, DV, DK), F32),
                   jax.ShapeDtypeStruct((nb, H, DK), F32),
                   jax.ShapeDtypeStruct((nb, H, LANES), F32)],
        grid=(nb, nc),
        in_specs=[pl.BlockSpec((L, qkv.shape[1]), row_map),
                  pl.BlockSpec((L, LANES), row_map),
                  pl.BlockSpec((L, hw), row_map),
                  pl.BlockSpec((1, hw), lambda b, c: (0, 0)),
                  pl.BlockSpec((1, H, DV, DK), st4),
                  pl.BlockSpec((1, H, DK), st3),
                  pl.BlockSpec((1, H, LANES), st3),
                  pl.BlockSpec(memory_space=pl.ANY)],
        out_specs=[pl.BlockSpec((L, hw), row_map),
                   pl.BlockSpec((1, H, DV, DK), st4),
                   pl.BlockSpec((1, H, DK), st3),
                   pl.BlockSpec((1, H, LANES), st3)],
        scratch_shapes=[pltpu.VMEM((H, MLSTM_EXT, DK), F32), pltpu.VMEM((H, SUBLANES, LANES), F32)],
        input_output_aliases={7: 0},
        compiler_params=_cparams(("parallel", "arbitrary")), name="mlstm",
    )(qkv, small, mo, norm_g, c0, n0, m0r, merged)
    merged, c_out, n_out, m_out = outs
    return merged, c_out, n_out, m_out[:, :, 0]


def _dsa_prompt_kernel(iq_ref, smq_ref, smk_ref, aq_ref, k_ref, v_ref, mrg_in_ref, out_ref,
                       ikx_ref, vt_ref, key_ref, sel_ref, acc_ref, m_ref, l_ref, *, T, QB, n_sel):
    del mrg_in_ref
    KC = QB
    NH, HD = ATTN_HEADS, ATTN_HD
    j = pl.program_id(1)
    nkc = j + 1

    @pl.when(j == 0)
    def _per_sequence():
        sm = smk_ref[...]
        lane = lax.broadcasted_iota(jnp.int32, (T, LANES), 1)
        ikx_ref[0:T, :] = jnp.where(lane < IDX_DIM, sm, 0.0).astype(BF16)
        ikx_ref[T:2 * T, :] = jnp.where(lane >= IDX_DIM, pltpu.roll(sm, IDX_DIM, 1), 0.0).astype(BF16)
        for c in range(T // KC):
            vt_ref[c] = v_ref[c * KC:(c + 1) * KC, :].astype(F32).T.astype(BF16)

    w_t = smq_ref[...].T * (IDX_HEADS * IDX_DIM) ** -0.5
    ri = lax.broadcasted_iota(jnp.int32, (KC, QB), 0)
    ci = lax.broadcasted_iota(jnp.int32, (KC, QB), 1)

    def score_chunk(c, carry):
        off = pl.multiple_of(c * KC, KC)
        acc = jnp.zeros((KC, QB), F32)
        for p in range(IDX_HEADS // 2):
            rhs = iq_ref[:, p * LANES:(p + 1) * LANES]
            d_even = _dot_nt(ikx_ref[pl.ds(off, KC), :], rhs)
            d_odd = _dot_nt(ikx_ref[pl.ds(T + off, KC), :], rhs)
            acc = acc + w_t[SM_IW + 2 * p:SM_IW + 2 * p + 1, :] * jnp.maximum(d_even, 0.0)
            acc = acc + w_t[SM_IW + 2 * p + 1:SM_IW + 2 * p + 2, :] * jnp.maximum(d_odd, 0.0)
        bits = pltpu.bitcast(acc, jnp.int32)
        key = jnp.where(bits < 0, bits ^ jnp.int32(0x7FFFFFFF), bits)
        visible = jnp.logical_or(ri <= ci, c < j)
        key_ref[pl.ds(off, KC), :] = jnp.where(visible, key, jnp.int32(INT_MIN))
        return carry

    lax.fori_loop(0, nkc, score_chunk, 0)

    def count(cand, strict):
        def body(c, a8):
            off = pl.multiple_of(c * KC, KC)
            kk = key_ref[pl.ds(off, KC), :]
            hit = jnp.where((kk > cand) if strict else (kk >= cand), 1, 0)
            for r in range(KC // SUBLANES):
                a8 = a8 + hit[r * SUBLANES:(r + 1) * SUBLANES, :]
            return a8
        a8 = lax.fori_loop(0, nkc, body, jnp.zeros((SUBLANES, QB), jnp.int32))
        return jnp.sum(a8, axis=0, keepdims=True)

    zero = jnp.zeros((1, QB), jnp.int32)
    thr0 = jnp.where(count(zero, False) >= n_sel, zero, jnp.int32(INT_MIN))

    def bit_body(i, thr):
        cand = thr | lax.shift_left(jnp.int32(1), 30 - i)
        return jnp.where(count(cand, False) >= n_sel, cand, thr)

    thr = lax.fori_loop(0, 31, bit_body, thr0)
    n_ties = (n_sel - count(thr, True)).astype(F32)
    has_thr = jnp.where(thr != jnp.int32(INT_MIN), 1.0, 0.0)
    ltri = jnp.where(ri >= ci, 1.0, 0.0).astype(BF16)

    def select_chunk(c, run):
        off = pl.multiple_of(c * KC, KC)
        kk = key_ref[pl.ds(off, KC), :]
        tie = jnp.where(kk == thr, has_thr, 0.0)
        rank = run + _dot(ltri, tie.astype(BF16))
        sel_ref[pl.ds(off, KC), :] = jnp.where(kk > thr, 1.0, jnp.where(rank <= n_ties, tie, 0.0))
        return rank[KC - 1:KC, :]

    lax.fori_loop(0, nkc, select_chunk, jnp.zeros((1, QB), F32))

    m_ref[...] = jnp.full((NH, QB), NEG_BIG, F32)
    l_ref[...] = jnp.zeros((NH, QB), F32)
    acc_ref[...] = jnp.zeros((NH, HD, QB), F32)
    scale = HD ** -0.5

    def attn_chunk(c, carry):
        off = pl.multiple_of(c * KC, KC)
        sel = sel_ref[pl.ds(off, KC), :] > 0.5
        for h in range(NH):
            kh = k_ref[pl.ds(off, KC), h * HD:(h + 1) * HD]
            lg = _dot_nt(kh, aq_ref[:, h * HD:(h + 1) * HD]) * scale
            m_old = m_ref[h:h + 1, :]
            m_new = jnp.maximum(m_old, jnp.max(jnp.where(sel, lg, NEG_BIG), axis=0, keepdims=True))
            alpha = jnp.exp(m_old - m_new)
            p = jnp.where(sel, jnp.exp(lg - m_new), 0.0)
            l_ref[h:h + 1, :] = alpha * l_ref[h:h + 1, :] + jnp.sum(p, axis=0, keepdims=True)
            acc_ref[h] = alpha * acc_ref[h] + _dot(vt_ref[c, h * HD:(h + 1) * HD, :], p.astype(BF16))
            m_ref[h:h + 1, :] = m_new
        return carry

    lax.fori_loop(0, nkc, attn_chunk, 0)
    for h in range(NH):
        o = acc_ref[h] / l_ref[h:h + 1, :]
        out_ref[:, h * HD:(h + 1) * HD] = o.T.astype(out_ref.dtype)


def _dsa_prompt(iq, small, aq, kb, vb, merged, *, nb, T, n_sel):
    QB = min(DSA_QB, T)
    nq = T // QB
    aw = ATTN_HEADS * ATTN_HD
    q_map = lambda b, j: (b * nq + j, 0)
    seq_map = lambda b, j: (b, 0)
    return pl.pallas_call(
        functools.partial(_dsa_prompt_kernel, T=T, QB=QB, n_sel=n_sel),
        out_shape=jax.ShapeDtypeStruct(merged.shape, merged.dtype),
        grid=(nb, nq),
        in_specs=[pl.BlockSpec((QB, IDX_HEADS * IDX_DIM), q_map),
                  pl.BlockSpec((QB, LANES), q_map),
                  pl.BlockSpec((T, LANES), seq_map),
                  pl.BlockSpec((QB, aw), q_map),
                  pl.BlockSpec((T, aw), seq_map),
                  pl.BlockSpec((T, aw), seq_map),
                  pl.BlockSpec(memory_space=pl.ANY)],
        out_specs=pl.BlockSpec((QB, aw), lambda b, j: (b * nq + j, 1)),
        scratch_shapes=[pltpu.VMEM((2 * T, LANES), BF16),
                        pltpu.VMEM((T // QB, aw, QB), BF16),
                        pltpu.VMEM((T, QB), jnp.int32),
                        pltpu.VMEM((T, QB), F32),
                        pltpu.VMEM((ATTN_HEADS, ATTN_HD, QB), F32),
                        pltpu.VMEM((ATTN_HEADS, QB), F32),
                        pltpu.VMEM((ATTN_HEADS, QB), F32)],
        input_output_aliases={6: 0},
        compiler_params=_cparams(("parallel", "arbitrary")), name="dsa_prompt",
    )(iq, small, small, aq, kb, vb, merged)


def _dsa_s_select_kernel(pt_ref, iqx_ref, wcol_ref, iknew_ref, *refs, PP, n_pages, n_sel, TN, TNP):
    del pt_ref
    page_refs = refs[:PP]
    mask_ref = refs[PP]
    key_ref = refs[PP + 1]
    s = pl.program_id(1)
    ns = pl.num_programs(1)
    HP = PAGE_SIZE // 2
    R = 2 * TNP * IDX_HEADS
    NG = 2 * TNP
    gi = lax.broadcasted_iota(jnp.int32, (NG, R), 0)
    gr = lax.broadcasted_iota(jnp.int32, (NG, R), 1)
    group = jnp.where(gr // IDX_HEADS == gi, 1.0, 0.0).astype(BF16)

    def page_keys(page):
        d = _dot_nt(iqx_ref[0], page.astype(BF16))
        x = jnp.maximum(d, 0.0) * wcol_ref[0][:, :HP]
        sc = _dot_exact_rhs(group, x)
        bits = pltpu.bitcast(sc, jnp.int32)
        return jnp.where(bits < 0, bits ^ jnp.int32(0x7FFFFFFF), bits)

    for pp in range(PP):
        key_ref[s * PP + pp] = page_keys(page_refs[pp][0])

    @pl.when(s == ns - 1)
    def _select():
        kn = page_keys(iknew_ref[0])
        gq = lax.broadcasted_iota(jnp.int32, (NG, HP), 0)
        jj = lax.broadcasted_iota(jnp.int32, (NG, HP), 1)
        newpos = 2 * jj + gq // TNP
        visible = jnp.logical_and(newpos <= gq % TNP, newpos < TN)
        key_ref[n_pages] = jnp.where(visible, kn, jnp.int32(INT_MIN))

        def count(cand8):
            cand = jnp.concatenate([cand8, cand8], axis=0)
            def body(p, a):
                return a + jnp.where(key_ref[p] >= cand, 1, 0)
            a = lax.fori_loop(0, n_pages + 1, body, jnp.zeros((NG, HP), jnp.int32))
            tot = jnp.sum(a, axis=1, keepdims=True)
            return tot[:TNP] + tot[TNP:]

        zero = jnp.zeros((TNP, 1), jnp.int32)
        thr0 = jnp.where(count(zero) >= n_sel, zero, jnp.int32(INT_MIN))

        def bit_body(i, thr):
            cand = thr | lax.shift_left(jnp.int32(1), 30 - i)
            return jnp.where(count(cand) >= n_sel, cand, thr)

        thr8 = lax.fori_loop(0, 31, bit_body, thr0)
        thr = jnp.concatenate([thr8, thr8], axis=0)

        def write(p, carry):
            kk = key_ref[p]
            keep = jnp.logical_and(kk >= thr, kk != jnp.int32(INT_MIN))
            mask_ref[0, p] = jnp.where(keep, 1.0, 0.0)
            return carry

        lax.fori_loop(0, n_pages + 1, write, 0)


def _dsa_s_select(page_table, iqx, wcol, iknew, idx_pages, *, n_sel, TN, TNP):
    DB, n_pages = page_table.shape
    PP = 8
    while n_pages % PP:
        PP //= 2
    HP = PAGE_SIZE // 2
    NG = 2 * TNP
    R = iqx.shape[1]
    page_specs = [pl.BlockSpec((1, HP, LANES), functools.partial(
        lambda b, s, pt, pp: (pt[b, s * PP + pp], 0, 0), pp=pp)) for pp in range(PP)]
    grid_spec = pltpu.PrefetchScalarGridSpec(
        num_scalar_prefetch=1, grid=(DB, n_pages // PP),
        in_specs=[pl.BlockSpec((1, R, LANES), lambda b, s, pt: (b, 0, 0)),
                  pl.BlockSpec((1, R, LANES), lambda b, s, pt: (b, 0, 0)),
                  pl.BlockSpec((1, HP, LANES), lambda b, s, pt: (b, 0, 0))] + page_specs,
        out_specs=pl.BlockSpec((1, n_pages + 1, NG, HP), lambda b, s, pt: (b, 0, 0, 0)),
        scratch_shapes=[pltpu.VMEM((n_pages + 1, NG, HP), jnp.int32)])
    return pl.pallas_call(
        functools.partial(_dsa_s_select_kernel, PP=PP, n_pages=n_pages, n_sel=n_sel, TN=TN, TNP=TNP),
        out_shape=jax.ShapeDtypeStruct((DB, n_pages + 1, NG, HP), F32),
        grid_spec=grid_spec,
        compiler_params=_cparams(("parallel", "arbitrary")), name="dsa_sample_select",
    )(page_table, iqx, wcol, iknew, *([idx_pages] * PP))


def _dsa_s_attn_kernel(pt_ref, q_ref, mask_ref, masknew_ref, knew_ref, vnew_ref, kp_ref, vp_ref,
                       mrg_in_ref, out_ref, qbd_ref, m_ref, l_ref, acc_ref, *, TN, TNP):
    del pt_ref, mrg_in_ref
    NH, HD = ATTN_HEADS, ATTN_HD
    AW = NH * HD
    RQ = TN * NH
    p = pl.program_id(1)
    npg = pl.num_programs(1)
    hrow = lax.broadcasted_iota(jnp.int32, (NH, AW), 0)
    hcol = lax.broadcasted_iota(jnp.int32, (NH, AW), 1) // HD
    own = hrow == hcol
    scale = HD ** -0.5

    @pl.when(p == 0)
    def _init():
        for t in range(TN):
            qt = jnp.broadcast_to(q_ref[0, t:t + 1, :].astype(F32), (NH, AW))
            qbd_ref[t * NH:(t + 1) * NH, :] = jnp.where(own, qt, 0.0).astype(BF16)
        m_ref[...] = jnp.full((RQ, LANES), NEG_BIG, F32)
        l_ref[...] = jnp.zeros((RQ, LANES), F32)
        acc_ref[...] = jnp.zeros((RQ, AW), F32)

    def process(kp, vp, mask):
        J = kp.shape[0]
        qbd = qbd_ref[...]
        lg, sel = [], []
        for par in range(2):
            lg.append(_dot_nt(qbd, kp[:, par * AW:(par + 1) * AW].astype(BF16)) * scale)
            rows = [jnp.broadcast_to(mask[par * TNP + t:par * TNP + t + 1, :], (NH, J)) for t in range(TN)]
            sel.append(jnp.concatenate(rows, axis=0) > 0.5)
        m_old = m_ref[:, 0:1]
        mx = jnp.maximum(jnp.max(jnp.where(sel[0], lg[0], NEG_BIG), axis=1, keepdims=True),
                         jnp.max(jnp.where(sel[1], lg[1], NEG_BIG), axis=1, keepdims=True))
        m_new = jnp.maximum(m_old, mx)
        alpha = jnp.exp(m_old - m_new)
        acc = alpha * acc_ref[...]
        lsum = alpha * l_ref[:, 0:1]
        for par in range(2):
            pr = jnp.where(sel[par], jnp.exp(lg[par] - m_new), 0.0)
            lsum = lsum + jnp.sum(pr, axis=1, keepdims=True)
            acc = acc + _dot(pr.astype(BF16), vp[:, par * AW:(par + 1) * AW].astype(BF16))
        acc_ref[...] = acc
        l_ref[...] = jnp.broadcast_to(lsum, (RQ, LANES))
        m_ref[...] = jnp.broadcast_to(m_new, (RQ, LANES))

    process(kp_ref[0], vp_ref[0], mask_ref[0, 0])

    @pl.when(p == npg - 1)
    def _fin():
        process(knew_ref[0], vnew_ref[0], masknew_ref[0, 0][:, :SUBLANES])
        o = acc_ref[...] / l_ref[:, 0:1]
        rows = []
        for t in range(TN):
            blk = jnp.where(own, o[t * NH:(t + 1) * NH, :], 0.0)
            rows.append(jnp.sum(blk, axis=0, keepdims=True))
        rows.append(jnp.zeros((TNP - TN, AW), F32))
        out_ref[0] = jnp.concatenate(rows, axis=0).astype(out_ref.dtype)


def _dsa_s_attn(page_table, q, mask, knew, vnew, k_pages, v_pages, merged, *, TN, TNP):
    DB, n_pages = page_table.shape
    HP = PAGE_SIZE // 2
    AW = ATTN_HEADS * ATTN_HD
    NG = 2 * TNP
    RQ = TN * ATTN_HEADS
    bmap = lambda b, p, pt: (b, 0, 0)
    grid_spec = pltpu.PrefetchScalarGridSpec(
        num_scalar_prefetch=1, grid=(DB, n_pages),
        in_specs=[pl.BlockSpec((1, TNP, AW), bmap),
                  pl.BlockSpec((1, 1, NG, HP), lambda b, p, pt: (b, p, 0, 0)),
                  pl.BlockSpec((1, 1, NG, HP), lambda b, p, pt: (b, n_pages, 0, 0)),
                  pl.BlockSpec((1, SUBLANES, 2 * AW), bmap),
                  pl.BlockSpec((1, SUBLANES, 2 * AW), bmap),
                  pl.BlockSpec((1, HP, 2 * AW), lambda b, p, pt: (pt[b, p], 0, 0)),
                  pl.BlockSpec((1, HP, 2 * AW), lambda b, p, pt: (pt[b, p], 0, 0)),
                  pl.BlockSpec(memory_space=pl.ANY)],
        out_specs=pl.BlockSpec((1, TNP, AW), lambda b, p, pt: (b, 0, 1)),
        scratch_shapes=[pltpu.VMEM((RQ, AW), BF16), pltpu.VMEM((RQ, LANES), F32),
                        pltpu.VMEM((RQ, LANES), F32), pltpu.VMEM((RQ, AW), F32)])
    return pl.pallas_call(
        functools.partial(_dsa_s_attn_kernel, TN=TN, TNP=TNP),
        out_shape=jax.ShapeDtypeStruct(merged.shape, merged.dtype),
        grid_spec=grid_spec,
        input_output_aliases={8: 0},
        compiler_params=_cparams(("parallel", "arbitrary")), name="dsa_sample_attn",
    )(page_table, q, mask, mask, knew, vnew, k_pages, v_pages, merged)


def _mm_res_ln_kernel(a_ref, w_ref, res_ref, g_ref, b_ref, of_ref, ob_ref, acc_ref, *, alpha, nk):
    k = pl.program_id(1)
    part = _dot(a_ref[...], w_ref[...])

    def finish(total):
        y = alpha * res_ref[...] + total
        mu = jnp.mean(y, axis=1, keepdims=True)
        var = jnp.mean(jnp.square(y - mu), axis=1, keepdims=True)
        out = (y - mu) * lax.rsqrt(var + LN_EPS) * g_ref[...] + b_ref[...]
        of_ref[...] = out
        ob_ref[...] = out.astype(ob_ref.dtype)

    if nk == 1:
        finish(part)
    else:
        @pl.when(k == 0)
        def _first():
            acc_ref[...] = part

        @pl.when(jnp.logical_and(k > 0, k < nk - 1))
        def _mid():
            acc_ref[...] += part

        @pl.when(k == nk - 1)
        def _last():
            finish(acc_ref[...] + part)


def _mm_res_ln(a, w, res, gamma, beta, *, alpha, tk=None, name="mm_res_ln"):
    M, K = a.shape
    N = w.shape[1]
    tm = min(512, M)
    tk = K if tk is None else tk
    assert M % tm == 0 and K % tk == 0
    nk = K // tk
    row = lambda m, k: (m, 0)
    return pl.pallas_call(
        functools.partial(_mm_res_ln_kernel, alpha=alpha, nk=nk),
        out_shape=[jax.ShapeDtypeStruct((M, N), F32), jax.ShapeDtypeStruct((M, N), BF16)],
        grid=(M // tm, nk),
        in_specs=[pl.BlockSpec((tm, tk), lambda m, k: (m, k)),
                  pl.BlockSpec((tk, N), lambda m, k: (k, 0)),
                  pl.BlockSpec((tm, N), row),
                  pl.BlockSpec((1, N), lambda m, k: (0, 0)),
                  pl.BlockSpec((1, N), lambda m, k: (0, 0))],
        out_specs=[pl.BlockSpec((tm, N), row), pl.BlockSpec((tm, N), row)],
        scratch_shapes=[pltpu.VMEM((tm, N) if nk > 1 else (SUBLANES, LANES), F32)],
        compiler_params=_cparams(("parallel", "arbitrary")), name=name,
    )(a, w, res, gamma, beta)


def _xattn_kernel(q_ref, mk_ref, mv_ref, o_ref):
    D = q_ref.shape[-1]
    hd = D // MEM_HEADS
    scale = hd ** -0.5
    for h in range(MEM_HEADS):
        q = q_ref[0, :, h * hd:(h + 1) * hd]
        kh = mk_ref[0, :, h * hd:(h + 1) * hd].astype(BF16)
        vh = mv_ref[0, :, h * hd:(h + 1) * hd].astype(BF16)
        lg = _dot_nt(q, kh) * scale
        e = jnp.exp(lg - jnp.max(lg, axis=1, keepdims=True))
        p = e / jnp.sum(e, axis=1, keepdims=True)
        o_ref[0, :, h * hd:(h + 1) * hd] = _dot(p.astype(BF16), vh).astype(o_ref.dtype)


def _xattn(q, mem_k, mem_v):
    nb, Tq, D = q.shape
    S = mem_k.shape[1]
    tq = min(512, Tq)
    assert Tq % tq == 0
    return pl.pallas_call(
        _xattn_kernel,
        out_shape=jax.ShapeDtypeStruct((nb, Tq, D), BF16),
        grid=(nb, Tq // tq),
        in_specs=[pl.BlockSpec((1, tq, D), lambda b, i: (b, i, 0)),
                  pl.BlockSpec((1, S, D), lambda b, i: (b, 0, 0)),
                  pl.BlockSpec((1, S, D), lambda b, i: (b, 0, 0))],
        out_specs=pl.BlockSpec((1, tq, D), lambda b, i: (b, i, 0)),
        compiler_params=_cparams(("parallel", "arbitrary")), name="xattn",
    )(q, mem_k, mem_v)


def _ffn_act_kernel(x_ref, wg_ref, wu_ref, o_ref):
    x = x_ref[...]
    gte = _dot(x, wg_ref[...])
    up = _dot(x, wu_ref[...])
    o_ref[...] = (gte * jax.nn.sigmoid(gte) * up).astype(o_ref.dtype)


def _ffn_tile(n):
    for t in (512, 256, 128):
        if n % t == 0:
            return t
    return n


def _ffn_act(x, wg, wu):
    M, K = x.shape
    N = wg.shape[1]
    tm = min(1024, M)
    tn = _ffn_tile(N)
    assert M % tm == 0
    return pl.pallas_call(
        _ffn_act_kernel,
        out_shape=jax.ShapeDtypeStruct((M, N), BF16),
        grid=(N // tn, M // tm),
        in_specs=[pl.BlockSpec((tm, K), lambda n, m: (m, 0)),
                  pl.BlockSpec((K, tn), lambda n, m: (0, n)),
                  pl.BlockSpec((K, tn), lambda n, m: (0, n))],
        out_specs=pl.BlockSpec((tm, tn), lambda n, m: (m, n)),
        compiler_params=_cparams(("parallel", "parallel")), name="ffn_act",
    )(x, wg, wu)


def _in_weights(w_in):
    H, DK, DV = MLSTM_HEADS, MLSTM_QK, MLSTM_V
    aw = ATTN_HEADS * ATTN_HD
    iw = IDX_HEADS * IDX_DIM
    o = 0
    w_qkv = w_in[:, o:o + 2 * H * DK + H * DV]; o += 2 * H * DK + H * DV
    w_mo = w_in[:, o:o + H * DV]; o += H * DV
    w_ig = w_in[:, o:o + H]; o += H
    w_fg = w_in[:, o:o + H]; o += H
    w_aq = w_in[:, o:o + aw]; o += aw
    w_ak = w_in[:, o:o + aw]; o += aw
    w_av = w_in[:, o:o + aw]; o += aw
    w_iq = w_in[:, o:o + iw]; o += iw
    w_ik = w_in[:, o:o + IDX_DIM]; o += IDX_DIM
    w_iw = w_in[:, o:o + IDX_HEADS]; o += IDX_HEADS
    assert o == w_in.shape[1]
    pad = jnp.zeros((w_in.shape[0], LANES - IDX_DIM - 2 * H - IDX_HEADS), w_in.dtype)
    w_small = jnp.concatenate([w_ik, w_ig, w_fg, w_iw, pad], axis=1)
    return tuple(w.astype(BF16) for w in (w_qkv, w_mo, w_aq, w_ak, w_av, w_iq, w_small))


def _mixer_inputs(xb, w, bias_small, pos, tag):
    w_qkv, w_mo, w_aq, w_ak, w_av, w_iq, w_small = w
    cos_a, sin_a = _rope_tables(pos, ATTN_HD)
    cos_i, sin_i = _rope_tables(pos, IDX_DIM)
    cos_s, sin_s = _small_tables(pos)
    (qkv,) = _proj(xb, w_qkv, [BF16], name=tag + "_proj_qkv")
    (mo,) = _proj(xb, w_mo, [F32], name=tag + "_proj_mo")
    (aq,) = _proj(xb, w_aq, [BF16], rope=1, cos=cos_a, sin=sin_a, name=tag + "_proj_aq")
    ak, akb = _proj(xb, w_ak, [F32, BF16], rope=1, cos=cos_a, sin=sin_a, name=tag + "_proj_ak")
    av, avb = _proj(xb, w_av, [F32, BF16], name=tag + "_proj_av")
    (iq,) = _proj(xb, w_iq, [BF16], rope=2, cos=cos_i, sin=sin_i, name=tag + "_proj_iq")
    (small,) = _proj(xb, w_small, [F32], rope=2, cos=cos_s, sin=sin_s, bias=bias_small,
                     name=tag + "_proj_small")
    return qkv, mo, aq, ak, akb, av, avb, iq, small


def _block_tail(x2d, merged, mem_k, mem_v, nb, w_out, ln1, w_cq, w_co, ln2, w_gate, w_up, w_down, ln3,
                alpha, tag):
    M, D = x2d.shape
    x1, x1b = _mm_res_ln(merged, w_out, x2d, ln1[0], ln1[1], alpha=alpha, name=tag + "_out_ln1")
    (cq,) = _proj(x1b, w_cq, [BF16], name=tag + "_proj_cq")
    ctx = _xattn(cq.reshape(nb, M // nb, D), mem_k, mem_v).reshape(M, D)
    x2, x2b = _mm_res_ln(ctx, w_co, x1, ln2[0], ln2[1], alpha=alpha, name=tag + "_co_ln2")
    act = _ffn_act(x2b, w_gate, w_up)
    F = act.shape[1]
    tk = F
    for cand in (1408, 1024, 512):
        if F % cand == 0 and F > cand:
            tk = cand
            break
    x3, _ = _mm_res_ln(act, w_down, x2, ln3[0], ln3[1], alpha=alpha, tk=tk, name=tag + "_down_ln3")
    return x3


def kernel(x_prompt, x_sample, cache_k, cache_v, cache_idx_k, state_C, state_n, state_m, cache_mem_k, cache_mem_v, page_table, mem_prompt, w_in, b_igate, b_fgate, mlstm_norm_g, w_out, ln1_g, ln1_b, w_cq, w_ck, w_cv, w_co, ln2_g, ln2_b, w_gate, w_up, w_down, ln3_g, ln3_b):
    B, T, D = x_prompt.shape
    DB, TN, _ = x_sample.shape
    depth = w_in.shape[0]
    n_pages = page_table.shape[1]
    past = n_pages * PAGE_SIZE
    n_mem = mem_prompt.shape[1]
    H, DK, DV = MLSTM_HEADS, MLSTM_QK, MLSTM_V
    AW = ATTN_HEADS * ATTN_HD
    MIX = H * DV + AW
    TNP = -(-TN // SUBLANES) * SUBLANES
    assert TNP == SUBLANES
    alpha = (2 * depth) ** 0.25
    n_sel_p = min(TOPK_MAX, T // 4)
    n_sel_s = min(TOPK_MAX, (past + TN) // 4)
    L = MLSTM_CHUNK if T % MLSTM_CHUNK == 0 else T
    page_table = page_table.astype(jnp.int32)

    pos_p = jnp.arange(T)
    pos_s = past + (jnp.arange(DB * TNP) % TNP)

    xp = x_prompt.reshape(B * T, D)
    xs = jnp.pad(x_sample, ((0, 0), (0, TNP - TN), (0, 0))).reshape(DB * TNP, D)
    new = [[] for _ in range(14)]
    row2 = lambda a: a.reshape(1, -1).astype(F32)

    for l in range(depth):
        w_l = _in_weights(w_in[l])
        zpad = jnp.zeros((LANES - SM_IW,), F32)
        bias_small = jnp.concatenate([jnp.zeros((SM_IG,), F32), b_igate[l].astype(F32),
                                      b_fgate[l].astype(F32), zpad]).reshape(1, LANES)
        norm_g = row2(mlstm_norm_g[l])
        w_out_b, w_cq_b, w_co_b = (w.astype(BF16) for w in (w_out[l], w_cq[l], w_co[l]))
        w_gate_b, w_up_b, w_down_b = (w.astype(BF16) for w in (w_gate[l], w_up[l], w_down[l]))
        ln1 = (row2(ln1_g[l]), row2(ln1_b[l]))
        ln2 = (row2(ln2_g[l]), row2(ln2_b[l]))
        ln3 = (row2(ln3_g[l]), row2(ln3_b[l]))
        tail_w = (w_out_b, ln1, w_cq_b, w_co_b, ln2, w_gate_b, w_up_b, w_down_b, ln3)

        qkv, mo, aq, ak, akb, av, avb, iq, small = _mixer_inputs(xp.astype(BF16), w_l, bias_small, pos_p, "p")
        merged = jnp.zeros((B * T, MIX), BF16)
        merged, pC, pn, pm = _mlstm(
            qkv, small, mo, norm_g, jnp.zeros((B, H, DV, DK), F32), jnp.zeros((B, H, DK), F32),
            jnp.zeros((B, H), F32), merged, nb=B, T=T, L=L, valid=L)
        merged = _dsa_prompt(iq, small, aq, akb, avb, merged, nb=B, T=T, n_sel=n_sel_p)
        memb = mem_prompt.reshape(B * n_mem, D).astype(BF16)
        (mem_k,) = _proj(memb, w_ck[l].astype(BF16), [F32], name="p_proj_ck")
        (mem_v,) = _proj(memb, w_cv[l].astype(BF16), [F32], name="p_proj_cv")
        mem_k = mem_k.reshape(B, n_mem, D)
        mem_v = mem_v.reshape(B, n_mem, D)
        xp = _block_tail(xp, merged, mem_k, mem_v, B, *tail_w, alpha, "p")

        sqkv, smo, saq, sak, sakb, sav, savb, siq, ssmall = _mixer_inputs(
            xs.astype(BF16), w_l, bias_small, pos_s, "s")
        smerged = jnp.zeros((DB * TNP, MIX), BF16)
        smerged, sC, sn, sm = _mlstm(qkv=sqkv, small=ssmall, mo=smo, norm_g=norm_g, c0=state_C[l].astype(F32),
                                     n0=state_n[l].astype(F32), m0=state_m[l].astype(F32), merged=smerged,
                                     nb=DB, T=TNP, L=TNP, valid=TN)
        siq3 = siq.reshape(DB, TNP * IDX_HEADS, IDX_DIM)
        zq = jnp.zeros_like(siq3)
        iqx = jnp.concatenate([jnp.concatenate([siq3, zq], axis=2),
                               jnp.concatenate([zq, siq3], axis=2)], axis=1)
        wq = ssmall[:, SM_IW:SM_IW + IDX_HEADS].reshape(DB, TNP * IDX_HEADS, 1) * (IDX_HEADS * IDX_DIM) ** -0.5
        wcol = jnp.broadcast_to(jnp.concatenate([wq, wq], axis=1), (DB, 2 * TNP * IDX_HEADS, LANES))
        HP = PAGE_SIZE // 2
        ik_new = ssmall[:, SM_IK:SM_IK + IDX_DIM].reshape(DB, TNP // 2, LANES)
        ik_new = jnp.pad(ik_new, ((0, 0), (0, HP - TNP // 2), (0, 0)))
        idx_pages = cache_idx_k[l].reshape(-1, HP, LANES)
        mask = _dsa_s_select(page_table, iqx, wcol, ik_new, idx_pages, n_sel=n_sel_s, TN=TN, TNP=TNP)
        knew = jnp.pad(sakb.reshape(DB, TNP // 2, 2 * AW), ((0, 0), (0, SUBLANES - TNP // 2), (0, 0)))
        vnew = jnp.pad(savb.reshape(DB, TNP // 2, 2 * AW), ((0, 0), (0, SUBLANES - TNP // 2), (0, 0)))
        smerged = _dsa_s_attn(page_table, saq.reshape(DB, TNP, AW), mask, knew, vnew,
                              cache_k[l].reshape(-1, HP, 2 * AW), cache_v[l].reshape(-1, HP, 2 * AW),
                              smerged.reshape(DB, TNP, MIX), TN=TN, TNP=TNP).reshape(DB * TNP, MIX)
        xs = _block_tail(xs, smerged, cache_mem_k[l].reshape(DB, n_mem, D), cache_mem_v[l].reshape(DB, n_mem, D),
                         DB, *tail_w, alpha, "s")

        def s_rows(a, shape):
            return a.reshape(DB, TNP, -1)[:, :TN].reshape(shape)

        vals = (ak.reshape(B, T, ATTN_HEADS, ATTN_HD), av.reshape(B, T, ATTN_HEADS, ATTN_HD),
                small[:, SM_IK:SM_IK + IDX_DIM].reshape(B, T, IDX_DIM), pC, pn, pm,
                mem_k.reshape(B, n_mem, MEM_HEADS, D // MEM_HEADS), mem_v.reshape(B, n_mem, MEM_HEADS, D // MEM_HEADS),
                s_rows(sak, (DB, TN, ATTN_HEADS, ATTN_HD)), s_rows(sav, (DB, TN, ATTN_HEADS, ATTN_HD)),
                s_rows(ssmall[:, SM_IK:SM_IK + IDX_DIM], (DB, TN, IDX_DIM)), sC, sn, sm)
        for slot, val in zip(new, vals):
            slot.append(val)

    outs = [jnp.stack(slot) for slot in new]
    y_prompt = xp.reshape(B, T, D)
    y_sample = xs.reshape(DB, TNP, D)[:, :TN]
    return (y_prompt, y_sample, *outs)
```

```python
import functools

import jax
import jax.numpy as jnp
from jax import lax
from jax.experimental import pallas as pl
from jax.experimental.pallas import tpu as pltpu

MLSTM_HEADS = 4
MLSTM_QK = 128
MLSTM_V = 256
ATTN_HEADS = 8
ATTN_HD = 128
IDX_HEADS = 16
IDX_DIM = 64
TOPK_MAX = 256
PAGE_SIZE = 128
MEM_HEADS = 4
ROPE_THETA = 10000.0
LN_EPS = 1e-5

LANES = 128
SUBLANES = 8
VMEM_LIMIT_BYTES = 52 * 1024 * 1024

SM_IK = 0
SM_IG = IDX_DIM
SM_FG = IDX_DIM + MLSTM_HEADS
SM_IW = IDX_DIM + 2 * MLSTM_HEADS

MLSTM_CHUNK = 256
DSA_QB = 256
MLSTM_EXT = MLSTM_V + LANES
NEG_BIG = -1e30
INT_MIN = -(2 ** 31)

F32 = jnp.float32
BF16 = jnp.bfloat16


def _cparams(sem):
    return pltpu.CompilerParams(dimension_semantics=sem, vmem_limit_bytes=VMEM_LIMIT_BYTES)


def _dot(a, b):
    return jnp.dot(a, b, preferred_element_type=F32)


def _dot_nt(a, b):
    return lax.dot_general(a, b, (((1,), (1,)), ((), ())), preferred_element_type=F32)


def _dot_tn(a, b):
    return lax.dot_general(a, b, (((0,), (0,)), ((), ())), preferred_element_type=F32)


def _split3(x):
    p0 = x.astype(BF16)
    r = x - p0.astype(F32)
    p1 = r.astype(BF16)
    r = r - p1.astype(F32)
    return p0, p1, r.astype(BF16)


def _dot_exact_rhs(a_b16, x_f32):
    p0, p1, p2 = _split3(x_f32)
    return _dot(a_b16, p0) + _dot(a_b16, p1) + _dot(a_b16, p2)


def _proj_kernel(*refs, rope, has_bias, n_out, tn):
    x_ref, w_ref = refs[0], refs[1]
    pos = 2
    cos_ref = sin_ref = bias_ref = None
    if rope:
        cos_ref, sin_ref = refs[pos], refs[pos + 1]
        pos += 2
    if has_bias:
        bias_ref = refs[pos]
        pos += 1
    out_refs = refs[pos:pos + n_out]
    acc = _dot(x_ref[...], w_ref[...])
    tm = acc.shape[0]
    if rope:
        cos = cos_ref[...]
        sin = sin_ref[...]
        lane = lax.broadcasted_iota(jnp.int32, (tm, LANES), 1)
    for j in range(tn // LANES):
        blk = acc[:, j * LANES:(j + 1) * LANES]
        if rope == 1:
            partner = pltpu.roll(blk, 64, 1)
            blk = blk * cos + partner * sin
        elif rope == 2:
            partner = jnp.where((lane % 64) < 32, pltpu.roll(blk, 96, 1), pltpu.roll(blk, 32, 1))
            blk = blk * cos + partner * sin
        if has_bias:
            blk = blk + bias_ref[...]
        for o in out_refs:
            o[:, j * LANES:(j + 1) * LANES] = blk.astype(o.dtype)


def _proj(x, w, out_dtypes, rope=0, cos=None, sin=None, bias=None, name="proj"):
    M, K = x.shape
    N = w.shape[1]
    tm = min(1024, M)
    tn = min(1024, N)
    assert M % tm == 0 and N % tn == 0 and tn % LANES == 0
    grid = (N // tn, M // tm)
    in_specs = [pl.BlockSpec((tm, K), lambda n, m: (m, 0)),
                pl.BlockSpec((K, tn), lambda n, m: (0, n))]
    args = [x, w]
    if rope:
        if cos.shape[0] < tm:
            assert tm % cos.shape[0] == 0
            cos = jnp.tile(cos, (tm // cos.shape[0], 1))
            sin = jnp.tile(sin, (tm // sin.shape[0], 1))
        rows = cos.shape[0]
        assert rows % tm == 0
        nrb = rows // tm
        tab_spec = pl.BlockSpec((tm, LANES), lambda n, m: (m % nrb, 0))
        in_specs += [tab_spec, tab_spec]
        args += [cos, sin]
    if bias is not None:
        assert tn == LANES
        in_specs.append(pl.BlockSpec((1, LANES), lambda n, m: (0, 0)))
        args.append(bias)
    out_shape = [jax.ShapeDtypeStruct((M, N), dt) for dt in out_dtypes]
    out_specs = [pl.BlockSpec((tm, tn), lambda n, m: (m, n)) for _ in out_dtypes]
    outs = pl.pallas_call(
        functools.partial(_proj_kernel, rope=rope, has_bias=bias is not None,
                          n_out=len(out_dtypes), tn=tn),
        out_shape=out_shape, grid=grid, in_specs=in_specs, out_specs=out_specs,
        compiler_params=_cparams(("parallel", "parallel")), name=name,
    )(*args)
    return outs


def _rope_tables(pos, head_dim):
    half = head_dim // 2
    inv_freq = ROPE_THETA ** (-jnp.arange(half, dtype=F32) / half)
    ang = pos.astype(F32)[:, None] * inv_freq
    cos, sin = jnp.cos(ang), jnp.sin(ang)
    reps = LANES // head_dim
    cos_full = jnp.tile(jnp.concatenate([cos, cos], axis=-1), (1, reps))
    sin_full = jnp.tile(jnp.concatenate([-sin, sin], axis=-1), (1, reps))
    return cos_full, sin_full


def _small_tables(pos):
    cos_i, sin_i = _rope_tables(pos, IDX_DIM)
    lane = jnp.arange(LANES)[None, :]
    return jnp.where(lane < IDX_DIM, cos_i, 1.0), jnp.where(lane < IDX_DIM, sin_i, 0.0)


def _mlstm_kernel(qkv_ref, g_ref, mo_ref, ng_ref, c0_ref, n0_ref, m0_ref,
                  hm_ref, cout_ref, nout_ref, mout_ref, cext_ref, m_ref, *, L, valid):
    H, DK, DV = MLSTM_HEADS, MLSTM_QK, MLSTM_V
    c = pl.program_id(1)
    nc = pl.num_programs(1)
    row8 = lax.broadcasted_iota(jnp.int32, (SUBLANES, DK), 0)

    @pl.when(c == 0)
    def _init():
        for h in range(H):
            cext_ref[h, 0:DV, :] = c0_ref[0, h]
            n_row = jnp.broadcast_to(n0_ref[0, h:h + 1, :], (SUBLANES, DK))
            cext_ref[h, DV:DV + SUBLANES, :] = jnp.where(row8 == 0, n_row, 0.0)
            cext_ref[h, DV + SUBLANES:, :] = jnp.zeros((MLSTM_EXT - DV - SUBLANES, DK), F32)
            m_ref[h] = jnp.broadcast_to(m0_ref[0, h:h + 1, :], (SUBLANES, LANES))

    scale = DK ** -0.5
    g = g_ref[...]
    row = lax.broadcasted_iota(jnp.int32, (L, 1), 0)
    ri = lax.broadcasted_iota(jnp.int32, (L, L), 0)
    ci = lax.broadcasted_iota(jnp.int32, (L, L), 1)
    causal = ci <= ri
    lt_b = jnp.where(causal, 1.0, 0.0).astype(BF16)
    ones_b = jnp.ones((L, L), BF16)
    lane_e = lax.broadcasted_iota(jnp.int32, (L, LANES), 1)
    one_col = jnp.where(lane_e == 0, 1.0, 0.0).astype(BF16)

    for h in range(H):
        q = qkv_ref[:, h * DK:(h + 1) * DK]
        k = qkv_ref[:, H * DK + h * DK:H * DK + (h + 1) * DK]
        v = qkv_ref[:, 2 * H * DK + h * DV:2 * H * DK + (h + 1) * DV]
        i_col = g[:, SM_IG + h:SM_IG + h + 1]
        lf_col = jax.nn.log_sigmoid(g[:, SM_FG + h:SM_FG + h + 1])
        if valid < L:
            i_col = jnp.where(row < valid, i_col, NEG_BIG)
            lf_col = jnp.where(row < valid, lf_col, 0.0)
        rhs = jnp.concatenate([jnp.where(ri > ci, lf_col, 0.0),
                               jnp.broadcast_to(lf_col, (L, LANES))], axis=1)
        cs = _dot_exact_rhs(lt_b, rhs)
        dsum = cs[:, :L]
        bcol = cs[:, L:L + 1]
        imat = _dot_exact_rhs(ones_b, jnp.where(ri == ci, i_col, 0.0))
        m_prev = m_ref[h, 0:1, 0:1]
        inter = bcol + m_prev
        dlog = jnp.where(causal, dsum + imat, -jnp.inf)
        m_t = jnp.maximum(inter, jnp.max(dlog, axis=1, keepdims=True))
        w_prev = jnp.exp(inter - m_t)
        s_mat = (_dot_nt(q, k) * scale) * jnp.exp(dlog - m_t)
        vext = jnp.concatenate([v, one_col], axis=1)
        cb = cext_ref[h].astype(BF16)
        numext = w_prev * _dot_nt(q, cb) + _dot(s_mat.astype(BF16), vext)
        num = numext[:, :DV]
        nq = numext[:, DV:DV + 1]
        hh = num / jnp.maximum(jnp.abs(nq), jnp.exp(-m_t))
        mu = jnp.mean(hh, axis=1, keepdims=True)
        var = jnp.mean(jnp.square(hh - mu), axis=1, keepdims=True)
        hn = (hh - mu) * lax.rsqrt(var + LN_EPS)
        gate = jax.nn.sigmoid(mo_ref[:, h * DV:(h + 1) * DV])
        hm_ref[:, h * DV:(h + 1) * DV] = (hn * ng_ref[:, h * DV:(h + 1) * DV] * gate).astype(hm_ref.dtype)
        m_new = m_t[L - 1:L, :]
        b_last = bcol[L - 1:L, :]
        decay = jnp.exp(b_last + m_prev - m_new)
        ws_col = jnp.exp(b_last - bcol + i_col - m_new) * scale
        xw = (ws_col * vext.astype(F32)).astype(BF16)
        cext_ref[h] = decay * cext_ref[h] + _dot_tn(xw, k)
        m_ref[h] = jnp.broadcast_to(m_new, (SUBLANES, LANES))

    @pl.when(c == nc - 1)
    def _fin():
        for h in range(H):
            cout_ref[0, h] = cext_ref[h, 0:DV, :]
            nout_ref[0, h:h + 1, :] = cext_ref[h, DV:DV + 1, :]
            mout_ref[0, h:h + 1, :] = m_ref[h, 0:1, :]


def _mlstm(qkv, small, mo, norm_g, c0, n0, m0, *, nb, T, L, valid):
    H, DK, DV = MLSTM_HEADS, MLSTM_QK, MLSTM_V
    nc = T // L
    hw = H * DV
    m0r = jnp.broadcast_to(m0[:, :, None], (nb, H, LANES)).astype(F32)
    row_map = lambda b, c: (b * nc + c, 0)
    st4 = lambda b, c: (b, 0, 0, 0)
    st3 = lambda b, c: (b, 0, 0)
    outs = pl.pallas_call(
        functools.partial(_mlstm_kernel, L=L, valid=valid),
        out_shape=[jax.ShapeDtypeStruct((nb * T, hw), BF16),
                   jax.ShapeDtypeStruct((nb, H, DV, DK), F32),
                   jax.ShapeDtypeStruct((nb, H, DK), F32),
                   jax.ShapeDtypeStruct((nb, H, LANES), F32)],
        grid=(nb, nc),
        in_specs=[pl.BlockSpec((L, qkv.shape[1]), row_map),
                  pl.BlockSpec((L, LANES), row_map),
                  pl.BlockSpec((L, hw), row_map),
                  pl.BlockSpec((1, hw), lambda b, c: (0, 0)),
                  pl.BlockSpec((1, H, DV, DK), st4),
                  pl.BlockSpec((1, H, DK), st3),
                  pl.BlockSpec((1, H, LANES), st3)],
        out_specs=[pl.BlockSpec((L, hw), row_map),
                   pl.BlockSpec((1, H, DV, DK), st4),
                   pl.BlockSpec((1, H, DK), st3),
                   pl.BlockSpec((1, H, LANES), st3)],
        scratch_shapes=[pltpu.VMEM((H, MLSTM_EXT, DK), F32), pltpu.VMEM((H, SUBLANES, LANES), F32)],
        compiler_params=_cparams(("parallel", "arbitrary")), name="mlstm",
    )(qkv, small, mo, norm_g, c0, n0, m0r)
    hm, c_out, n_out, m_out = outs
    return hm, c_out, n_out, m_out[:, :, 0]


def _dsa_prompt_kernel(iq_ref, smq_ref, smk_ref, aq_ref, k_ref, v_ref, out_ref,
                       ikx_ref, vt_ref, key_ref, bias_ref, acc_ref, *, T, QB, n_sel):
    KC = QB
    NH, HD = ATTN_HEADS, ATTN_HD
    j = pl.program_id(1)
    nkc = j + 1

    @pl.when(j == 0)
    def _per_sequence():
        sm = smk_ref[...]
        lane = lax.broadcasted_iota(jnp.int32, (T, LANES), 1)
        ikx_ref[0:T, :] = jnp.where(lane < IDX_DIM, sm, 0.0).astype(BF16)
        ikx_ref[T:2 * T, :] = jnp.where(lane >= IDX_DIM, pltpu.roll(sm, IDX_DIM, 1), 0.0).astype(BF16)
        for c in range(T // KC):
            vt_ref[c] = v_ref[c * KC:(c + 1) * KC, :].astype(F32).T.astype(BF16)

    w_t = smq_ref[...].T * (IDX_HEADS * IDX_DIM) ** -0.5
    ri = lax.broadcasted_iota(jnp.int32, (KC, QB), 0)
    ci = lax.broadcasted_iota(jnp.int32, (KC, QB), 1)

    def score_chunk(c, carry):
        off = pl.multiple_of(c * KC, KC)
        acc = jnp.zeros((KC, QB), F32)
        for p in range(IDX_HEADS // 2):
            rhs = iq_ref[:, p * LANES:(p + 1) * LANES]
            d_even = _dot_nt(ikx_ref[pl.ds(off, KC), :], rhs)
            d_odd = _dot_nt(ikx_ref[pl.ds(T + off, KC), :], rhs)
            acc = acc + w_t[SM_IW + 2 * p:SM_IW + 2 * p + 1, :] * jnp.maximum(d_even, 0.0)
            acc = acc + w_t[SM_IW + 2 * p + 1:SM_IW + 2 * p + 2, :] * jnp.maximum(d_odd, 0.0)
        bits = pltpu.bitcast(acc, jnp.int32)
        key = jnp.where(bits < 0, bits ^ jnp.int32(0x7FFFFFFF), bits)
        visible = jnp.logical_or(ri <= ci, c < j)
        key_ref[pl.ds(off, KC), :] = jnp.where(visible, key, jnp.int32(INT_MIN))
        return carry

    lax.fori_loop(0, nkc, score_chunk, 0)

    def count(cand, strict):
        def body(c, a8):
            off = pl.multiple_of(c * KC, KC)
            kk = key_ref[pl.ds(off, KC), :]
            hit = jnp.where((kk > cand) if strict else (kk >= cand), 1, 0)
            for r in range(KC // SUBLANES):
                a8 = a8 + hit[r * SUBLANES:(r + 1) * SUBLANES, :]
            return a8
        a8 = lax.fori_loop(0, nkc, body, jnp.zeros((SUBLANES, QB), jnp.int32))
        return jnp.sum(a8, axis=0, keepdims=True)

    zero = jnp.zeros((1, QB), jnp.int32)
    thr0 = jnp.where(count(zero, False) >= n_sel, zero, jnp.int32(INT_MIN))

    def bit_body(i, thr):
        cand = thr | lax.shift_left(jnp.int32(1), 30 - i)
        return jnp.where(count(cand, False) >= n_sel, cand, thr)

    thr = lax.fori_loop(0, 31, bit_body, thr0)
    n_ties = (n_sel - count(thr, True)).astype(F32)
    has_thr = jnp.where(thr != jnp.int32(INT_MIN), 1.0, 0.0)
    ltri = jnp.where(ri >= ci, 1.0, 0.0).astype(BF16)

    def select_chunk(c, run):
        off = pl.multiple_of(c * KC, KC)
        kk = key_ref[pl.ds(off, KC), :]
        tie = jnp.where(kk == thr, has_thr, 0.0)
        rank = run + _dot(ltri, tie.astype(BF16))
        keep = jnp.where(kk > thr, 1.0, jnp.where(rank <= n_ties, tie, 0.0))
        bias_ref[pl.ds(off, KC), :] = (1.0 - keep) * NEG_BIG
        return rank[KC - 1:KC, :]

    lax.fori_loop(0, nkc, select_chunk, jnp.zeros((1, QB), F32))

    acc_ref[...] = jnp.zeros((NH, HD, QB), F32)
    scale = HD ** -0.5

    def attn_chunk(c, carry):
        ms, ls = carry
        off = pl.multiple_of(c * KC, KC)
        bias = bias_ref[pl.ds(off, KC), :]
        new_m, new_l = [], []
        for h in range(NH):
            kh = k_ref[pl.ds(off, KC), h * HD:(h + 1) * HD]
            lg = _dot_nt(kh, aq_ref[:, h * HD:(h + 1) * HD]) * scale + bias
            m_new = jnp.maximum(ms[h], jnp.max(lg, axis=0, keepdims=True))
            alpha = jnp.exp(ms[h] - m_new)
            p = jnp.exp(lg - m_new)
            new_l.append(alpha * ls[h] + jnp.sum(p, axis=0, keepdims=True))
            acc_ref[h] = alpha * acc_ref[h] + _dot(vt_ref[c, h * HD:(h + 1) * HD, :], p.astype(BF16))
            new_m.append(m_new)
        return tuple(new_m), tuple(new_l)

    init = (tuple(jnp.full((1, QB), NEG_BIG, F32) for _ in range(NH)),
            tuple(jnp.zeros((1, QB), F32) for _ in range(NH)))
    _, ls = lax.fori_loop(0, nkc, attn_chunk, init)
    for h in range(NH):
        o = acc_ref[h] / ls[h]
        out_ref[:, h * HD:(h + 1) * HD] = o.T.astype(out_ref.dtype)


def _dsa_prompt(iq, small, aq, kb, vb, *, nb, T, n_sel):
    QB = min(DSA_QB, T)
    nq = T // QB
    aw = ATTN_HEADS * ATTN_HD
    q_map = lambda b, j: (b * nq + j, 0)
    seq_map = lambda b, j: (b, 0)
    return pl.pallas_call(
        functools.partial(_dsa_prompt_kernel, T=T, QB=QB, n_sel=n_sel),
        out_shape=jax.ShapeDtypeStruct((nb * T, aw), BF16),
        grid=(nb, nq),
        in_specs=[pl.BlockSpec((QB, IDX_HEADS * IDX_DIM), q_map),
                  pl.BlockSpec((QB, LANES), q_map),
                  pl.BlockSpec((T, LANES), seq_map),
                  pl.BlockSpec((QB, aw), q_map),
                  pl.BlockSpec((T, aw), seq_map),
                  pl.BlockSpec((T, aw), seq_map)],
        out_specs=pl.BlockSpec((QB, aw), q_map),
        scratch_shapes=[pltpu.VMEM((2 * T, LANES), BF16),
                        pltpu.VMEM((T // QB, aw, QB), BF16),
                        pltpu.VMEM((T, QB), jnp.int32),
                        pltpu.VMEM((T, QB), F32),
                        pltpu.VMEM((ATTN_HEADS, ATTN_HD, QB), F32)],
        compiler_params=_cparams(("parallel", "arbitrary")), name="dsa_prompt",
    )(iq, small, small, aq, kb, vb)


def _dsa_s_select_kernel(pt_ref, iq_ref, wcol_ref, iknew_ref, ptcol_ref, *refs, PP, n_pages, n_sel, TN, TNP):
    del pt_ref
    page_refs = refs[:PP]
    idx_ref, npast_ref, selnew_ref, key_ref = refs[PP:PP + 4]
    s = pl.program_id(1)
    ns = pl.num_programs(1)
    NPG = key_ref.shape[0]
    PS = PAGE_SIZE
    iq = iq_ref[0]
    wcol = wcol_ref[0]

    def tile_keys(d):
        x = jnp.maximum(d, 0.0) * wcol
        sc = x[0:TNP]
        for h in range(1, IDX_HEADS):
            sc = sc + x[h * TNP:(h + 1) * TNP]
        bits = pltpu.bitcast(sc, jnp.int32)
        return jnp.where(bits < 0, bits ^ jnp.int32(0x7FFFFFFF), bits)

    pages = jnp.concatenate([r[0, 0] for r in page_refs], axis=0).astype(BF16)
    dots = _dot_nt(iq, pages)
    for pp in range(PP):
        key_ref[s * PP + pp] = tile_keys(dots[:, pp * PS:(pp + 1) * PS])

    @pl.when(s == ns - 1)
    def _select():
        kn = tile_keys(_dot_nt(iq, iknew_ref[0].astype(BF16)))
        tq = lax.broadcasted_iota(jnp.int32, (TNP, PS), 0)
        jn = lax.broadcasted_iota(jnp.int32, (TNP, PS), 1)
        visible = jnp.logical_and(jn <= tq, jn < TN)
        key_ref[n_pages] = jnp.where(visible, kn, jnp.int32(INT_MIN))
        for p in range(n_pages + 1, NPG):
            key_ref[p] = jnp.full((TNP, PS), INT_MIN, jnp.int32)

        def count(cand, strict):
            def body(p, a):
                kk = key_ref[p]
                return a + jnp.where((kk > cand) if strict else (kk >= cand), 1, 0)
            a = lax.fori_loop(0, n_pages + 1, body, jnp.zeros((TNP, PS), jnp.int32))
            return jnp.broadcast_to(jnp.sum(a, axis=1, keepdims=True), (TNP, PS))

        zero = jnp.zeros((TNP, PS), jnp.int32)
        thr0 = jnp.where(count(zero, False) >= n_sel, zero, jnp.int32(INT_MIN))

        def bit_body(i, thr):
            cand = thr | lax.shift_left(jnp.int32(1), 30 - i)
            return jnp.where(count(cand, False) >= n_sel, cand, thr)

        thr = lax.fori_loop(0, 31, bit_body, thr0)
        n_ties = (n_sel - count(thr, True)).astype(F32)
        has_thr = jnp.where(thr != jnp.int32(INT_MIN), 1.0, 0.0)

        r_k = lax.broadcasted_iota(jnp.int32, (PS, PS), 0)
        c_k = lax.broadcasted_iota(jnp.int32, (PS, PS), 1)
        u_incl = jnp.where(r_k <= c_k, 1.0, 0.0).astype(BF16)
        u_strict = jnp.where(r_k < c_k, 1.0, 0.0).astype(BF16)
        ones_k = jnp.ones((PS, PS), BF16)
        r_g = lax.broadcasted_iota(jnp.int32, (NPG, NPG), 0)
        c_g = lax.broadcasted_iota(jnp.int32, (NPG, NPG), 1)
        l_strict = jnp.where(c_g < r_g, 1.0, 0.0).astype(BF16)
        r_p = lax.broadcasted_iota(jnp.int32, (n_pages, n_pages), 0)
        c_p = lax.broadcasted_iota(jnp.int32, (n_pages, n_pages), 1)
        l_incl = jnp.where(c_p <= r_p, 1.0, 0.0).astype(BF16)
        rep = n_sel // LANES
        slot = lax.broadcasted_iota(jnp.int32, (n_pages, n_sel), 1).astype(F32)
        jcol = lax.broadcasted_iota(jnp.int32, (PS, n_sel), 0).astype(F32)
        wide = lambda a: jnp.concatenate([a] * rep, axis=1)
        pt_w = wide(ptcol_ref[0])

        for t in range(TN):
            kt = key_ref[:, t, :]
            thr_t = thr[t:t + 1, :]
            tie = jnp.where(kt == thr_t, has_thr[t:t + 1, :], 0.0)
            tie_b = tie.astype(BF16)
            row_ties = _dot(tie_b, ones_k)
            rank = _dot(l_strict, row_ties.astype(BF16)) + _dot(tie_b, u_incl)
            sel = jnp.where(kt > thr_t, 1.0, jnp.where(rank <= n_ties[t:t + 1, :], tie, 0.0))
            selnew_ref[0, t:t + 1, :] = sel[n_pages:n_pages + 1, :]
            sp = sel[0:n_pages]
            sp_b = sp.astype(BF16)
            within = _dot(sp_b, u_strict)
            cnt = _dot(sp_b, ones_k)
            end = _dot(l_incl, cnt.astype(BF16))
            off_w, end_w = wide(end - cnt), wide(end)
            a = jnp.where(off_w <= slot, jnp.where(slot < end_w, 1.0, 0.0), 0.0)
            valid = jnp.sum(a, axis=0, keepdims=True)
            local = slot[0:1] - jnp.sum(a * off_w, axis=0, keepdims=True)
            phys = jnp.sum(a * pt_w, axis=0, keepdims=True)
            wsel = _dot_tn((sp * (within + 1.0)).astype(BF16), a.astype(BF16))
            jrow = jnp.sum(jnp.where(wsel == local + 1.0, jcol, 0.0), axis=0, keepdims=True)
            row = jnp.where(valid > 0.5, phys * PS + jrow, 0.0)
            idx_ref[0, t:t + 1, :] = row.astype(jnp.int32)
            n_row = jnp.broadcast_to(jnp.sum(valid, axis=1, keepdims=True), (1, LANES))
            npast_ref[0, t:t + 1, :] = n_row.astype(jnp.int32)
        if TNP > TN:
            idx_ref[0, TN:TNP, :] = jnp.zeros((TNP - TN, n_sel), jnp.int32)
            npast_ref[0, TN:TNP, :] = jnp.zeros((TNP - TN, LANES), jnp.int32)
            selnew_ref[0, TN:TNP, :] = jnp.zeros((TNP - TN, LANES), F32)


def _dsa_s_select(page_table, iq_hq, wcol, iknew, pt_col, cache_idx_k, *, layer, n_sel, TN, TNP):
    DB, n_pages = page_table.shape
    assert n_sel % LANES == 0 and PAGE_SIZE == LANES
    PP = 8
    while n_pages % PP:
        PP //= 2
    R = iq_hq.shape[1]
    NPG = -(-(n_pages + 1) // SUBLANES) * SUBLANES
    bmap = lambda b, s, pt: (b, 0, 0)
    page_specs = [pl.BlockSpec((1, 1, PAGE_SIZE, IDX_DIM), functools.partial(
        lambda b, s, pt, pp: (layer, pt[b, s * PP + pp], 0, 0), pp=pp)) for pp in range(PP)]
    grid_spec = pltpu.PrefetchScalarGridSpec(
        num_scalar_prefetch=1, grid=(DB, n_pages // PP),
        in_specs=[pl.BlockSpec((1, R, IDX_DIM), bmap),
                  pl.BlockSpec((1, R, LANES), bmap),
                  pl.BlockSpec((1, PAGE_SIZE, IDX_DIM), bmap),
                  pl.BlockSpec((1, n_pages, LANES), bmap)] + page_specs,
        out_specs=[pl.BlockSpec((1, TNP, n_sel), bmap),
                   pl.BlockSpec((1, TNP, LANES), bmap),
                   pl.BlockSpec((1, TNP, LANES), bmap)],
        scratch_shapes=[pltpu.VMEM((NPG, TNP, PAGE_SIZE), jnp.int32)])
    return pl.pallas_call(
        functools.partial(_dsa_s_select_kernel, PP=PP, n_pages=n_pages, n_sel=n_sel, TN=TN, TNP=TNP),
        out_shape=[jax.ShapeDtypeStruct((DB, TNP, n_sel), jnp.int32),
                   jax.ShapeDtypeStruct((DB, TNP, LANES), jnp.int32),
                   jax.ShapeDtypeStruct((DB, TNP, LANES), F32)],
        grid_spec=grid_spec,
        compiler_params=_cparams(("parallel", "arbitrary")), name="dsa_sample_select",
    )(page_table, iq_hq, wcol, iknew, pt_col, *([cache_idx_k] * PP))


def _dsa_s_attn_kernel(idx_ref, idxn_ref, npast_ref, q_ref, knew_ref, vnew_ref, selnew_ref, ck_hbm, cv_hbm,
                       out_ref, kbuf, vbuf, sem, *, layer, TN, n_sel):
    NH, HD = ATTN_HEADS, ATTN_HD
    TNP = q_ref.shape[1]
    b = pl.program_id(0)
    nb = pl.num_programs(0)
    scale = HD ** -0.5
    page_shift = PAGE_SIZE.bit_length() - 1

    def row_copies(row, slot, i):
        pg = lax.shift_right_logical(row, page_shift)
        off = row & (PAGE_SIZE - 1)
        return (pltpu.make_async_copy(ck_hbm.at[layer, pg, off], kbuf.at[slot, i], sem.at[0, slot]),
                pltpu.make_async_copy(cv_hbm.at[layer, pg, off], vbuf.at[slot, i], sem.at[1, slot]))

    def issue(rows_ref, t, slot):
        def body(i, carry):
            ck, cv = row_copies(rows_ref[0, t, i], slot, i)
            ck.start()
            cv.start()
            return carry
        lax.fori_loop(0, n_sel, body, 0)

    def wait(slot):
        def body(i, carry):
            ck, cv = row_copies(jnp.int32(0), slot, i)
            ck.wait()
            cv.wait()
            return carry
        lax.fori_loop(0, n_sel, body, 0)

    def head_tile(ref, t):
        return jnp.concatenate([ref[0, t:t + 1, h * HD:(h + 1) * HD] for h in range(NH)], axis=0).astype(F32)

    def attend(t, slot):
        qt = head_tile(q_ref, t)
        kb = kbuf[slot]
        lg = jnp.sum(kb * qt[None], axis=-1, keepdims=True) * scale
        live = lax.broadcasted_iota(jnp.int32, (n_sel, NH, 1), 0) < npast_ref[0, t, 0]
        lg = jnp.where(live, lg, NEG_BIG)
        m = jnp.max(lg, axis=0)
        new = []
        for j in range(TN):
            keep = selnew_ref[0, t:t + 1, j:j + 1] > 0.5
            lj = jnp.sum(head_tile(knew_ref, j) * qt, axis=-1, keepdims=True) * scale
            lj = jnp.where(keep, lj, NEG_BIG)
            m = jnp.maximum(m, lj)
            new.append((keep, lj))
        p = jnp.where(live, jnp.exp(lg - m[None]), 0.0)
        den = jnp.sum(p, axis=0)
        o = jnp.sum(p * vbuf[slot], axis=0)
        for j, (keep, lj) in enumerate(new):
            pj = jnp.where(keep, jnp.exp(lj - m), 0.0)
            den = den + pj
            o = o + pj * head_tile(vnew_ref, j)
        o = o / den
        for h in range(NH):
            out_ref[0, t:t + 1, h * HD:(h + 1) * HD] = o[h:h + 1, :].astype(out_ref.dtype)

    @pl.when(b == 0)
    def _first():
        issue(idx_ref, 0, 0)

    if TNP > TN:
        out_ref[0, TN:TNP, :] = jnp.zeros((TNP - TN, NH * HD), out_ref.dtype)
    for t in range(TN):
        slot = t % 2
        if t + 1 < TN:
            issue(idx_ref, t + 1, 1 - slot)
        else:
            @pl.when(b + 1 < nb)
            def _next():
                issue(idxn_ref, 0, 1 - slot)
        wait(slot)
        attend(t, slot)


def _dsa_s_attn(sel_idx, n_past, sel_new, q, knew, vnew, cache_k, cache_v, *, layer, TN):
    DB, TNP, n_sel = sel_idx.shape
    assert TN % 2 == 0
    AW = ATTN_HEADS * ATTN_HD
    bmap = lambda b: (b, 0, 0)
    nxt = lambda b: (jnp.minimum(b + 1, DB - 1), 0, 0)
    smem = pltpu.SMEM
    return pl.pallas_call(
        functools.partial(_dsa_s_attn_kernel, layer=layer, TN=TN, n_sel=n_sel),
        out_shape=jax.ShapeDtypeStruct((DB, TNP, AW), BF16),
        grid=(DB,),
        in_specs=[pl.BlockSpec((1, TNP, n_sel), bmap, memory_space=smem),
                  pl.BlockSpec((1, TNP, n_sel), nxt, memory_space=smem),
                  pl.BlockSpec((1, TNP, LANES), bmap, memory_space=smem),
                  pl.BlockSpec((1, TNP, AW), bmap),
                  pl.BlockSpec((1, TNP, AW), bmap),
                  pl.BlockSpec((1, TNP, AW), bmap),
                  pl.BlockSpec((1, TNP, LANES), bmap),
                  pl.BlockSpec(memory_space=pl.ANY),
                  pl.BlockSpec(memory_space=pl.ANY)],
        out_specs=pl.BlockSpec((1, TNP, AW), bmap),
        scratch_shapes=[pltpu.VMEM((2, n_sel, ATTN_HEADS, ATTN_HD), F32),
                        pltpu.VMEM((2, n_sel, ATTN_HEADS, ATTN_HD), F32),
                        pltpu.SemaphoreType.DMA((2, 2))],
        compiler_params=_cparams(("arbitrary",)), name="dsa_sample_attn",
    )(sel_idx, sel_idx, n_past, q, knew, vnew, sel_new, cache_k, cache_v)


def _mm_res_ln_kernel(*refs, alpha, nk, n_a):
    a_refs = refs[:n_a]
    w_ref, res_ref, g_ref, b_ref, of_ref, ob_ref, acc_ref = refs[n_a:]
    k = pl.program_id(1)

    def finish(total):
        y = alpha * res_ref[...] + total
        mu = jnp.mean(y, axis=1, keepdims=True)
        var = jnp.mean(jnp.square(y - mu), axis=1, keepdims=True)
        out = (y - mu) * lax.rsqrt(var + LN_EPS) * g_ref[...] + b_ref[...]
        of_ref[...] = out
        ob_ref[...] = out.astype(ob_ref.dtype)

    def step(a_ref, first, last):
        part = _dot(a_ref[...], w_ref[...])
        if first and last:
            finish(part)
        elif first:
            acc_ref[...] = part
        elif last:
            finish(acc_ref[...] + part)
        else:
            acc_ref[...] += part

    if nk == 1:
        step(a_refs[0], True, True)
    elif n_a > 1:
        for i, a_ref in enumerate(a_refs):
            pl.when(k == i)(functools.partial(step, a_ref, i == 0, i == n_a - 1))
    else:
        pl.when(k == 0)(functools.partial(step, a_refs[0], True, False))
        pl.when(jnp.logical_and(k > 0, k < nk - 1))(functools.partial(step, a_refs[0], False, False))
        pl.when(k == nk - 1)(functools.partial(step, a_refs[0], False, True))


def _mm_res_ln(a_list, w, res, gamma, beta, *, alpha, tk=None, name="mm_res_ln"):
    M = a_list[0].shape[0]
    N = w.shape[1]
    tm = min(512, M)
    n_a = len(a_list)
    if n_a > 1:
        tk = a_list[0].shape[1]
        assert all(a.shape == (M, tk) for a in a_list) and tk * n_a == w.shape[0]
        nk = n_a
        a_specs = [pl.BlockSpec((tm, tk), lambda m, k: (m, 0)) for _ in a_list]
    else:
        K = a_list[0].shape[1]
        tk = K if tk is None else tk
        assert K % tk == 0 and K == w.shape[0]
        nk = K // tk
        a_specs = [pl.BlockSpec((tm, tk), lambda m, k: (m, k))]
    assert M % tm == 0
    row = lambda m, k: (m, 0)
    return pl.pallas_call(
        functools.partial(_mm_res_ln_kernel, alpha=alpha, nk=nk, n_a=n_a),
        out_shape=[jax.ShapeDtypeStruct((M, N), F32), jax.ShapeDtypeStruct((M, N), BF16)],
        grid=(M // tm, nk),
        in_specs=a_specs + [pl.BlockSpec((tk, N), lambda m, k: (k, 0)),
                            pl.BlockSpec((tm, N), row),
                            pl.BlockSpec((1, N), lambda m, k: (0, 0)),
                            pl.BlockSpec((1, N), lambda m, k: (0, 0))],
        out_specs=[pl.BlockSpec((tm, N), row), pl.BlockSpec((tm, N), row)],
        scratch_shapes=[pltpu.VMEM((tm, N) if nk > 1 else (SUBLANES, LANES), F32)],
        compiler_params=_cparams(("parallel", "arbitrary")), name=name,
    )(*a_list, w, res, gamma, beta)


def _xattn_kernel(q_ref, mk_ref, mv_ref, o_ref):
    D = q_ref.shape[-1]
    hd = D // MEM_HEADS
    scale = hd ** -0.5
    for h in range(MEM_HEADS):
        q = q_ref[0, :, h * hd:(h + 1) * hd]
        kh = mk_ref[0, :, h * hd:(h + 1) * hd].astype(BF16)
        vh = mv_ref[0, :, h * hd:(h + 1) * hd].astype(BF16)
        lg = _dot_nt(q, kh) * scale
        e = jnp.exp(lg - jnp.max(lg, axis=1, keepdims=True))
        p = e / jnp.sum(e, axis=1, keepdims=True)
        o_ref[0, :, h * hd:(h + 1) * hd] = _dot(p.astype(BF16), vh).astype(o_ref.dtype)


def _xattn(q, mem_k, mem_v):
    nb, Tq, D = q.shape
    S = mem_k.shape[1]
    tq = min(512, Tq)
    assert Tq % tq == 0
    return pl.pallas_call(
        _xattn_kernel,
        out_shape=jax.ShapeDtypeStruct((nb, Tq, D), BF16),
        grid=(nb, Tq // tq),
        in_specs=[pl.BlockSpec((1, tq, D), lambda b, i: (b, i, 0)),
                  pl.BlockSpec((1, S, D), lambda b, i: (b, 0, 0)),
                  pl.BlockSpec((1, S, D), lambda b, i: (b, 0, 0))],
        out_specs=pl.BlockSpec((1, tq, D), lambda b, i: (b, i, 0)),
        compiler_params=_cparams(("parallel", "arbitrary")), name="xattn",
    )(q, mem_k, mem_v)


def _ffn_act_kernel(x_ref, wg_ref, wu_ref, o_ref):
    x = x_ref[...]
    gte = _dot(x, wg_ref[...])
    up = _dot(x, wu_ref[...])
    o_ref[...] = (gte * jax.nn.sigmoid(gte) * up).astype(o_ref.dtype)


def _ffn_tile(n):
    for t in (512, 256, 128):
        if n % t == 0:
            return t
    return n


def _ffn_act(x, wg, wu):
    M, K = x.shape
    N = wg.shape[1]
    tm = min(1024, M)
    tn = _ffn_tile(N)
    assert M % tm == 0
    return pl.pallas_call(
        _ffn_act_kernel,
        out_shape=jax.ShapeDtypeStruct((M, N), BF16),
        grid=(N // tn, M // tm),
        in_specs=[pl.BlockSpec((tm, K), lambda n, m: (m, 0)),
                  pl.BlockSpec((K, tn), lambda n, m: (0, n)),
                  pl.BlockSpec((K, tn), lambda n, m: (0, n))],
        out_specs=pl.BlockSpec((tm, tn), lambda n, m: (m, n)),
        compiler_params=_cparams(("parallel", "parallel")), name="ffn_act",
    )(x, wg, wu)


def _in_weights(w_in):
    H, DK, DV = MLSTM_HEADS, MLSTM_QK, MLSTM_V
    aw = ATTN_HEADS * ATTN_HD
    iw = IDX_HEADS * IDX_DIM
    o = 0
    w_qkv = w_in[:, o:o + 2 * H * DK + H * DV]; o += 2 * H * DK + H * DV
    w_mo = w_in[:, o:o + H * DV]; o += H * DV
    w_ig = w_in[:, o:o + H]; o += H
    w_fg = w_in[:, o:o + H]; o += H
    w_aq = w_in[:, o:o + aw]; o += aw
    w_ak = w_in[:, o:o + aw]; o += aw
    w_av = w_in[:, o:o + aw]; o += aw
    w_iq = w_in[:, o:o + iw]; o += iw
    w_ik = w_in[:, o:o + IDX_DIM]; o += IDX_DIM
    w_iw = w_in[:, o:o + IDX_HEADS]; o += IDX_HEADS
    assert o == w_in.shape[1]
    pad = jnp.zeros((w_in.shape[0], LANES - IDX_DIM - 2 * H - IDX_HEADS), w_in.dtype)
    w_small = jnp.concatenate([w_ik, w_ig, w_fg, w_iw, pad], axis=1)
    return tuple(w.astype(BF16) for w in (w_qkv, w_mo, w_aq, w_ak, w_av, w_iq, w_small))


def _mixer_inputs(xb, w, bias_small, pos, tag):
    w_qkv, w_mo, w_aq, w_ak, w_av, w_iq, w_small = w
    cos_a, sin_a = _rope_tables(pos, ATTN_HD)
    cos_i, sin_i = _rope_tables(pos, IDX_DIM)
    cos_s, sin_s = _small_tables(pos)
    (qkv,) = _proj(xb, w_qkv, [BF16], name=tag + "_proj_qkv")
    (mo,) = _proj(xb, w_mo, [F32], name=tag + "_proj_mo")
    (aq,) = _proj(xb, w_aq, [BF16], rope=1, cos=cos_a, sin=sin_a, name=tag + "_proj_aq")
    ak, akb = _proj(xb, w_ak, [F32, BF16], rope=1, cos=cos_a, sin=sin_a, name=tag + "_proj_ak")
    av, avb = _proj(xb, w_av, [F32, BF16], name=tag + "_proj_av")
    (iq,) = _proj(xb, w_iq, [BF16], rope=2, cos=cos_i, sin=sin_i, name=tag + "_proj_iq")
    (small,) = _proj(xb, w_small, [F32], rope=2, cos=cos_s, sin=sin_s, bias=bias_small,
                     name=tag + "_proj_small")
    return qkv, mo, aq, ak, akb, av, avb, iq, small


def _block_tail(x2d, heads, mem_k, mem_v, nb, w_out, ln1, w_cq, w_co, ln2, w_gate, w_up, w_down, ln3,
                alpha, tag):
    M, D = x2d.shape
    x1, x1b = _mm_res_ln(heads, w_out, x2d, ln1[0], ln1[1], alpha=alpha, name=tag + "_out_ln1")
    (cq,) = _proj(x1b, w_cq, [BF16], name=tag + "_proj_cq")
    ctx = _xattn(cq.reshape(nb, M // nb, D), mem_k, mem_v).reshape(M, D)
    x2, x2b = _mm_res_ln([ctx], w_co, x1, ln2[0], ln2[1], alpha=alpha, name=tag + "_co_ln2")
    act = _ffn_act(x2b, w_gate, w_up)
    F = act.shape[1]
    tk = F
    for cand in (1408, 1024, 512):
        if F % cand == 0 and F > cand:
            tk = cand
            break
    x3, _ = _mm_res_ln([act], w_down, x2, ln3[0], ln3[1], alpha=alpha, tk=tk, name=tag + "_down_ln3")
    return x3


def kernel(x_prompt, x_sample, cache_k, cache_v, cache_idx_k, state_C, state_n, state_m, cache_mem_k, cache_mem_v, page_table, mem_prompt, w_in, b_igate, b_fgate, mlstm_norm_g, w_out, ln1_g, ln1_b, w_cq, w_ck, w_cv, w_co, ln2_g, ln2_b, w_gate, w_up, w_down, ln3_g, ln3_b):
    B, T, D = x_prompt.shape
    DB, TN, _ = x_sample.shape
    depth = w_in.shape[0]
    n_pages = page_table.shape[1]
    past = n_pages * PAGE_SIZE
    n_mem = mem_prompt.shape[1]
    H, DK, DV = MLSTM_HEADS, MLSTM_QK, MLSTM_V
    AW = ATTN_HEADS * ATTN_HD
    MIX = H * DV + AW
    TNP = -(-TN // SUBLANES) * SUBLANES
    assert TNP == SUBLANES
    alpha = (2 * depth) ** 0.25
    n_sel_p = min(TOPK_MAX, T // 4)
    n_sel_s = min(TOPK_MAX, (past + TN) // 4)
    L = MLSTM_CHUNK if T % MLSTM_CHUNK == 0 else T
    page_table = page_table.astype(jnp.int32)

    pos_p = jnp.arange(T)
    pos_s = past + (jnp.arange(DB * TNP) % TNP)

    xp = x_prompt.reshape(B * T, D)
    xs = jnp.pad(x_sample, ((0, 0), (0, TNP - TN), (0, 0))).reshape(DB * TNP, D)
    new = [[] for _ in range(14)]
    row2 = lambda a: a.reshape(1, -1).astype(F32)

    for l in range(depth):
        w_l = _in_weights(w_in[l])
        zpad = jnp.zeros((LANES - SM_IW,), F32)
        bias_small = jnp.concatenate([jnp.zeros((SM_IG,), F32), b_igate[l].astype(F32),
                                      b_fgate[l].astype(F32), zpad]).reshape(1, LANES)
        norm_g = row2(mlstm_norm_g[l])
        w_out_b, w_cq_b, w_co_b = (w.astype(BF16) for w in (w_out[l], w_cq[l], w_co[l]))
        w_gate_b, w_up_b, w_down_b = (w.astype(BF16) for w in (w_gate[l], w_up[l], w_down[l]))
        ln1 = (row2(ln1_g[l]), row2(ln1_b[l]))
        ln2 = (row2(ln2_g[l]), row2(ln2_b[l]))
        ln3 = (row2(ln3_g[l]), row2(ln3_b[l]))
        tail_w = (w_out_b, ln1, w_cq_b, w_co_b, ln2, w_gate_b, w_up_b, w_down_b, ln3)

        qkv, mo, aq, ak, akb, av, avb, iq, small = _mixer_inputs(xp.astype(BF16), w_l, bias_small, pos_p, "p")
        hm, pC, pn, pm = _mlstm(
            qkv, small, mo, norm_g, jnp.zeros((B, H, DV, DK), F32), jnp.zeros((B, H, DK), F32),
            jnp.zeros((B, H), F32), nb=B, T=T, L=L, valid=L)
        ha = _dsa_prompt(iq, small, aq, akb, avb, nb=B, T=T, n_sel=n_sel_p)
        memb = mem_prompt.reshape(B * n_mem, D).astype(BF16)
        (mem_k,) = _proj(memb, w_ck[l].astype(BF16), [F32], name="p_proj_ck")
        (mem_v,) = _proj(memb, w_cv[l].astype(BF16), [F32], name="p_proj_cv")
        mem_k = mem_k.reshape(B, n_mem, D)
        mem_v = mem_v.reshape(B, n_mem, D)
        xp = _block_tail(xp, [hm, ha], mem_k, mem_v, B, *tail_w, alpha, "p")

        sqkv, smo, saq, sak, sakb, sav, savb, siq, ssmall = _mixer_inputs(
            xs.astype(BF16), w_l, bias_small, pos_s, "s")
        shm, sC, sn, sm = _mlstm(sqkv, ssmall, smo, norm_g, state_C[l].astype(F32), state_n[l].astype(F32),
                                 state_m[l].astype(F32), nb=DB, T=TNP, L=TNP, valid=TN)
        iq_hq = siq.reshape(DB, TNP, IDX_HEADS, IDX_DIM).transpose(0, 2, 1, 3).reshape(DB, IDX_HEADS * TNP, IDX_DIM)
        w_hq = ssmall[:, SM_IW:SM_IW + IDX_HEADS].reshape(DB, TNP, IDX_HEADS).transpose(0, 2, 1)
        w_hq = w_hq.reshape(DB, IDX_HEADS * TNP, 1) * (IDX_HEADS * IDX_DIM) ** -0.5
        wcol = jnp.broadcast_to(w_hq, (DB, IDX_HEADS * TNP, LANES))
        ik_new = jnp.pad(ssmall[:, SM_IK:SM_IK + IDX_DIM].reshape(DB, TNP, IDX_DIM),
                         ((0, 0), (0, PAGE_SIZE - TNP), (0, 0)))
        pt_col = jnp.broadcast_to(page_table.astype(F32)[:, :, None], (DB, n_pages, LANES))
        sel_idx, n_past, sel_new = _dsa_s_select(page_table, iq_hq, wcol, ik_new, pt_col, cache_idx_k,
                                                 layer=l, n_sel=n_sel_s, TN=TN, TNP=TNP)
        sha = _dsa_s_attn(sel_idx, n_past, sel_new, saq.reshape(DB, TNP, AW), sakb.reshape(DB, TNP, AW),
                          savb.reshape(DB, TNP, AW), cache_k, cache_v, layer=l, TN=TN).reshape(DB * TNP, AW)
        xs = _block_tail(xs, [shm, sha], cache_mem_k[l].reshape(DB, n_mem, D), cache_mem_v[l].reshape(DB, n_mem, D),
                         DB, *tail_w, alpha, "s")

        def s_rows(a, shape):
            return a.reshape(DB, TNP, -1)[:, :TN].reshape(shape)

        vals = (ak.reshape(B, T, ATTN_HEADS, ATTN_HD), av.reshape(B, T, ATTN_HEADS, ATTN_HD),
                small[:, SM_IK:SM_IK + IDX_DIM].reshape(B, T, IDX_DIM), pC, pn, pm,
                mem_k.reshape(B, n_mem, MEM_HEADS, D // MEM_HEADS), mem_v.reshape(B, n_mem, MEM_HEADS, D // MEM_HEADS),
                s_rows(sak, (DB, TN, ATTN_HEADS, ATTN_HD)), s_rows(sav, (DB, TN, ATTN_HEADS, ATTN_HD)),
                s_rows(ssmall[:, SM_IK:SM_IK + IDX_DIM], (DB, TN, IDX_DIM)), sC, sn, sm)
        for slot, val in zip(new, vals):
            slot.append(val)

    outs = [jnp.stack(slot) for slot in new]
    y_prompt = xp.reshape(B, T, D)
    y_sample = xs.reshape(DB, TNP, D)[:, :TN]
    return (y_prompt, y_sample, *outs)
```

```python
import functools

import jax
import jax.numpy as jnp
from jax import lax
from jax.experimental import pallas as pl
from jax.experimental.pallas import tpu as pltpu

MLSTM_HEADS = 4
MLSTM_QK = 128
MLSTM_V = 256
ATTN_HEADS = 8
ATTN_HD = 128
IDX_HEADS = 16
IDX_DIM = 64
TOPK_MAX = 256
PAGE_SIZE = 128
MEM_HEADS = 4
ROPE_THETA = 10000.0
LN_EPS = 1e-5

LANES = 128
SUBLANES = 8
VMEM_LIMIT_BYTES = 52 * 1024 * 1024

SM_IK = 0
SM_IG = IDX_DIM
SM_FG = IDX_DIM + MLSTM_HEADS
SM_IW = IDX_DIM + 2 * MLSTM_HEADS

MLSTM_CHUNK = 256
DSA_QB = 256
DSA_KA = 512
MLSTM_EXT = MLSTM_V + LANES
NEG_BIG = -1e30
INT_MIN = -(2 ** 31)

F32 = jnp.float32
BF16 = jnp.bfloat16


def _cparams(sem):
    return pltpu.CompilerParams(dimension_semantics=sem, vmem_limit_bytes=VMEM_LIMIT_BYTES)


def _dot(a, b):
    return jnp.dot(a, b, preferred_element_type=F32)


def _dot_nt(a, b):
    return lax.dot_general(a, b, (((1,), (1,)), ((), ())), preferred_element_type=F32)


def _dot_tn(a, b):
    return lax.dot_general(a, b, (((0,), (0,)), ((), ())), preferred_element_type=F32)


def _split3(x):
    p0 = x.astype(BF16)
    r = x - p0.astype(F32)
    p1 = r.astype(BF16)
    r = r - p1.astype(F32)
    return p0, p1, r.astype(BF16)


def _dot_exact_rhs(a_b16, x_f32):
    p0, p1, p2 = _split3(x_f32)
    return _dot(a_b16, p0) + _dot(a_b16, p1) + _dot(a_b16, p2)


def _proj_kernel(*refs, rope, has_bias, n_out, tn):
    x_ref, w_ref = refs[0], refs[1]
    pos = 2
    cos_ref = sin_ref = bias_ref = None
    if rope:
        cos_ref, sin_ref = refs[pos], refs[pos + 1]
        pos += 2
    if has_bias:
        bias_ref = refs[pos]
        pos += 1
    out_refs = refs[pos:pos + n_out]
    acc = _dot(x_ref[...], w_ref[...])
    tm = acc.shape[0]
    if rope:
        cos = cos_ref[...]
        sin = sin_ref[...]
        lane = lax.broadcasted_iota(jnp.int32, (tm, LANES), 1)
    for j in range(tn // LANES):
        blk = acc[:, j * LANES:(j + 1) * LANES]
        if rope == 1:
            partner = pltpu.roll(blk, 64, 1)
            blk = blk * cos + partner * sin
        elif rope == 2:
            partner = jnp.where((lane % 64) < 32, pltpu.roll(blk, 96, 1), pltpu.roll(blk, 32, 1))
            blk = blk * cos + partner * sin
        if has_bias:
            blk = blk + bias_ref[...]
        for o in out_refs:
            o[:, j * LANES:(j + 1) * LANES] = blk.astype(o.dtype)


def _proj(x, w, out_dtypes, rope=0, cos=None, sin=None, bias=None, name="proj"):
    M, K = x.shape
    N = w.shape[1]
    tm = min(1024, M)
    tn = min(1024, N)
    assert M % tm == 0 and N % tn == 0 and tn % LANES == 0
    grid = (N // tn, M // tm)
    in_specs = [pl.BlockSpec((tm, K), lambda n, m: (m, 0)),
                pl.BlockSpec((K, tn), lambda n, m: (0, n))]
    args = [x, w]
    if rope:
        if cos.shape[0] < tm:
            assert tm % cos.shape[0] == 0
            cos = jnp.tile(cos, (tm // cos.shape[0], 1))
            sin = jnp.tile(sin, (tm // sin.shape[0], 1))
        rows = cos.shape[0]
        assert rows % tm == 0
        nrb = rows // tm
        tab_spec = pl.BlockSpec((tm, LANES), lambda n, m: (m % nrb, 0))
        in_specs += [tab_spec, tab_spec]
        args += [cos, sin]
    if bias is not None:
        assert tn == LANES
        in_specs.append(pl.BlockSpec((1, LANES), lambda n, m: (0, 0)))
        args.append(bias)
    out_shape = [jax.ShapeDtypeStruct((M, N), dt) for dt in out_dtypes]
    out_specs = [pl.BlockSpec((tm, tn), lambda n, m: (m, n)) for _ in out_dtypes]
    outs = pl.pallas_call(
        functools.partial(_proj_kernel, rope=rope, has_bias=bias is not None,
                          n_out=len(out_dtypes), tn=tn),
        out_shape=out_shape, grid=grid, in_specs=in_specs, out_specs=out_specs,
        compiler_params=_cparams(("parallel", "parallel")), name=name,
    )(*args)
    return outs


def _rope_tables(pos, head_dim):
    half = head_dim // 2
    inv_freq = ROPE_THETA ** (-jnp.arange(half, dtype=F32) / half)
    ang = pos.astype(F32)[:, None] * inv_freq
    cos, sin = jnp.cos(ang), jnp.sin(ang)
    reps = LANES // head_dim
    cos_full = jnp.tile(jnp.concatenate([cos, cos], axis=-1), (1, reps))
    sin_full = jnp.tile(jnp.concatenate([-sin, sin], axis=-1), (1, reps))
    return cos_full, sin_full


def _small_tables(pos):
    cos_i, sin_i = _rope_tables(pos, IDX_DIM)
    lane = jnp.arange(LANES)[None, :]
    return jnp.where(lane < IDX_DIM, cos_i, 1.0), jnp.where(lane < IDX_DIM, sin_i, 0.0)


def _mlstm_kernel(qkv_ref, g_ref, mo_ref, ng_ref, c0_ref, n0_ref, m0_ref,
                  hm_ref, cout_ref, nout_ref, mout_ref, cext_ref, m_ref, *, L, valid):
    H, DK, DV = MLSTM_HEADS, MLSTM_QK, MLSTM_V
    c = pl.program_id(1)
    nc = pl.num_programs(1)
    row8 = lax.broadcasted_iota(jnp.int32, (SUBLANES, DK), 0)

    @pl.when(c == 0)
    def _init():
        for h in range(H):
            cext_ref[h, 0:DV, :] = c0_ref[0, h]
            n_row = jnp.broadcast_to(n0_ref[0, h:h + 1, :], (SUBLANES, DK))
            cext_ref[h, DV:DV + SUBLANES, :] = jnp.where(row8 == 0, n_row, 0.0)
            cext_ref[h, DV + SUBLANES:, :] = jnp.zeros((MLSTM_EXT - DV - SUBLANES, DK), F32)
            m_ref[h] = jnp.broadcast_to(m0_ref[0, h:h + 1, :], (SUBLANES, LANES))

    scale = DK ** -0.5
    g = g_ref[...]
    row = lax.broadcasted_iota(jnp.int32, (L, 1), 0)
    ri = lax.broadcasted_iota(jnp.int32, (L, L), 0)
    ci = lax.broadcasted_iota(jnp.int32, (L, L), 1)
    causal = ci <= ri
    lt_b = jnp.where(causal, 1.0, 0.0).astype(BF16)
    ones_b = jnp.ones((L, L), BF16)
    lane_e = lax.broadcasted_iota(jnp.int32, (L, LANES), 1)
    one_col = jnp.where(lane_e == 0, 1.0, 0.0).astype(BF16)

    for h in range(H):
        q = qkv_ref[:, h * DK:(h + 1) * DK]
        k = qkv_ref[:, H * DK + h * DK:H * DK + (h + 1) * DK]
        v = qkv_ref[:, 2 * H * DK + h * DV:2 * H * DK + (h + 1) * DV]
        i_col = g[:, SM_IG + h:SM_IG + h + 1]
        lf_col = jax.nn.log_sigmoid(g[:, SM_FG + h:SM_FG + h + 1])
        if valid < L:
            i_col = jnp.where(row < valid, i_col, NEG_BIG)
            lf_col = jnp.where(row < valid, lf_col, 0.0)
        rhs = jnp.concatenate([jnp.where(ri > ci, lf_col, 0.0),
                               jnp.broadcast_to(lf_col, (L, LANES))], axis=1)
        cs = _dot_exact_rhs(lt_b, rhs)
        dsum = cs[:, :L]
        bcol = cs[:, L:L + 1]
        imat = _dot_exact_rhs(ones_b, jnp.where(ri == ci, i_col, 0.0))
        m_prev = m_ref[h, 0:1, 0:1]
        inter = bcol + m_prev
        dlog = jnp.where(causal, dsum + imat, -jnp.inf)
        m_t = jnp.maximum(inter, jnp.max(dlog, axis=1, keepdims=True))
        w_prev = jnp.exp(inter - m_t)
        s_mat = (_dot_nt(q, k) * scale) * jnp.exp(dlog - m_t)
        vext = jnp.concatenate([v, one_col], axis=1)
        cb = cext_ref[h].astype(BF16)
        numext = w_prev * _dot_nt(q, cb) + _dot(s_mat.astype(BF16), vext)
        num = numext[:, :DV]
        nq = numext[:, DV:DV + 1]
        hh = num / jnp.maximum(jnp.abs(nq), jnp.exp(-m_t))
        mu = jnp.mean(hh, axis=1, keepdims=True)
        var = jnp.mean(jnp.square(hh - mu), axis=1, keepdims=True)
        hn = (hh - mu) * lax.rsqrt(var + LN_EPS)
        gate = jax.nn.sigmoid(mo_ref[:, h * DV:(h + 1) * DV])
        hm_ref[:, h * DV:(h + 1) * DV] = (hn * ng_ref[:, h * DV:(h + 1) * DV] * gate).astype(hm_ref.dtype)
        m_new = m_t[L - 1:L, :]
        b_last = bcol[L - 1:L, :]
        decay = jnp.exp(b_last + m_prev - m_new)
        ws_col = jnp.exp(b_last - bcol + i_col - m_new) * scale
        xw = (ws_col * vext.astype(F32)).astype(BF16)
        cext_ref[h] = decay * cext_ref[h] + _dot_tn(xw, k)
        m_ref[h] = jnp.broadcast_to(m_new, (SUBLANES, LANES))

    @pl.when(c == nc - 1)
    def _fin():
        for h in range(H):
            cout_ref[0, h] = cext_ref[h, 0:DV, :]
            nout_ref[0, h:h + 1, :] = cext_ref[h, DV:DV + 1, :]
            mout_ref[0, h:h + 1, :] = m_ref[h, 0:1, :]


def _mlstm(qkv, small, mo, norm_g, c0, n0, m0, *, nb, T, L, valid):
    H, DK, DV = MLSTM_HEADS, MLSTM_QK, MLSTM_V
    nc = T // L
    hw = H * DV
    m0r = jnp.broadcast_to(m0[:, :, None], (nb, H, LANES)).astype(F32)
    row_map = lambda b, c: (b * nc + c, 0)
    st4 = lambda b, c: (b, 0, 0, 0)
    st3 = lambda b, c: (b, 0, 0)
    outs = pl.pallas_call(
        functools.partial(_mlstm_kernel, L=L, valid=valid),
        out_shape=[jax.ShapeDtypeStruct((nb * T, hw), BF16),
                   jax.ShapeDtypeStruct((nb, H, DV, DK), F32),
                   jax.ShapeDtypeStruct((nb, H, DK), F32),
                   jax.ShapeDtypeStruct((nb, H, LANES), F32)],
        grid=(nb, nc),
        in_specs=[pl.BlockSpec((L, qkv.shape[1]), row_map),
                  pl.BlockSpec((L, LANES), row_map),
                  pl.BlockSpec((L, hw), row_map),
                  pl.BlockSpec((1, hw), lambda b, c: (0, 0)),
                  pl.BlockSpec((1, H, DV, DK), st4),
                  pl.BlockSpec((1, H, DK), st3),
                  pl.BlockSpec((1, H, LANES), st3)],
        out_specs=[pl.BlockSpec((L, hw), row_map),
                   pl.BlockSpec((1, H, DV, DK), st4),
                   pl.BlockSpec((1, H, DK), st3),
                   pl.BlockSpec((1, H, LANES), st3)],
        scratch_shapes=[pltpu.VMEM((H, MLSTM_EXT, DK), F32), pltpu.VMEM((H, SUBLANES, LANES), F32)],
        compiler_params=_cparams(("parallel", "arbitrary")), name="mlstm",
    )(qkv, small, mo, norm_g, c0, n0, m0r)
    hm, c_out, n_out, m_out = outs
    return hm, c_out, n_out, m_out[:, :, 0]


def _dsa_prompt_kernel(iq_ref, smq_ref, smk_ref, aq_ref, k_ref, v_ref, out_ref,
                       ikx_ref, vt_ref, key_ref, hi_ref, lo_ref, bias_ref, acc_ref, *, T, QB, KA, n_sel):
    KC = QB
    RA = KA // KC
    NH, HD = ATTN_HEADS, ATTN_HD
    j = pl.program_id(1)
    nkc = j + 1

    @pl.when(j == 0)
    def _per_sequence():
        sm = smk_ref[...]
        lane = lax.broadcasted_iota(jnp.int32, (T, LANES), 1)
        ikx_ref[0:T, :] = jnp.where(lane < IDX_DIM, sm, 0.0).astype(BF16)
        ikx_ref[T:2 * T, :] = jnp.where(lane >= IDX_DIM, pltpu.roll(sm, IDX_DIM, 1), 0.0).astype(BF16)
        for c in range(T // KA):
            vt_ref[c] = v_ref[c * KA:(c + 1) * KA, :].astype(F32).T.astype(BF16)

    w_t = smq_ref[...].T * (IDX_HEADS * IDX_DIM) ** -0.5
    ri = lax.broadcasted_iota(jnp.int32, (KC, QB), 0)
    ci = lax.broadcasted_iota(jnp.int32, (KC, QB), 1)

    def score_chunk(c, carry):
        off = pl.multiple_of(c * KC, KC)
        acc = jnp.zeros((KC, QB), F32)
        for p in range(IDX_HEADS // 2):
            rhs = iq_ref[:, p * LANES:(p + 1) * LANES]
            d_even = _dot_nt(ikx_ref[pl.ds(off, KC), :], rhs)
            d_odd = _dot_nt(ikx_ref[pl.ds(T + off, KC), :], rhs)
            acc = acc + w_t[SM_IW + 2 * p:SM_IW + 2 * p + 1, :] * jnp.maximum(d_even, 0.0)
            acc = acc + w_t[SM_IW + 2 * p + 1:SM_IW + 2 * p + 2, :] * jnp.maximum(d_odd, 0.0)
        bits = pltpu.bitcast(acc, jnp.int32)
        key = jnp.where(bits < 0, bits ^ jnp.int32(0x7FFFFFFF), bits)
        visible = jnp.logical_or(ri <= ci, c < j)
        key = jnp.where(visible, key, jnp.int32(INT_MIN))
        key_ref[pl.ds(off, KC), :] = key
        hi_ref[pl.ds(off, KC), :] = lax.shift_right_arithmetic(key, 16).astype(jnp.int16)
        return carry

    lax.fori_loop(0, nkc, score_chunk, 0)

    HALF_MIN = -(2 ** 15)
    PK = 2 * SUBLANES

    def count_half(ref, cand, strict=False):
        c16 = jnp.broadcast_to(cand, (PK, QB)).astype(jnp.int16)
        one = jnp.ones((PK, QB), jnp.int16)
        zero16 = jnp.zeros((PK, QB), jnp.int16)

        def body(c, a):
            off = pl.multiple_of(c * KC, KC)
            kk = ref[pl.ds(off, KC), :]
            for r in range(KC // PK):
                blk = kk[r * PK:(r + 1) * PK, :]
                a = a + jnp.where((blk > c16) if strict else (blk >= c16), one, zero16)
            return a
        a = lax.fori_loop(0, nkc, body, zero16)
        return jnp.sum(a.astype(jnp.int32), axis=0, keepdims=True)

    def search_half(ref, need):
        zero = jnp.zeros((1, QB), jnp.int32)
        t0 = jnp.where(count_half(ref, zero) >= need, zero, jnp.int32(HALF_MIN))

        def bit_body(i, t):
            cand = t | lax.shift_left(jnp.int32(1), 14 - i)
            return jnp.where(count_half(ref, cand) >= need, cand, t)
        return lax.fori_loop(0, 15, bit_body, t0)

    thr_hi = search_half(hi_ref, n_sel)
    need_lo = n_sel - count_half(hi_ref, thr_hi, strict=True)

    def low_chunk(c, carry):
        off = pl.multiple_of(c * KC, KC)
        kk = key_ref[pl.ds(off, KC), :]
        lo = (kk & jnp.int32(0xFFFF)) + HALF_MIN
        same = lax.shift_right_arithmetic(kk, 16) == thr_hi
        lo_ref[pl.ds(off, KC), :] = jnp.where(same, lo, jnp.int32(HALF_MIN)).astype(jnp.int16)
        return carry

    lax.fori_loop(0, nkc, low_chunk, 0)
    thr = thr_hi * 65536 + (search_half(lo_ref, need_lo) - HALF_MIN)

    def count(cand, strict):
        def body(c, a8):
            off = pl.multiple_of(c * KC, KC)
            kk = key_ref[pl.ds(off, KC), :]
            hit = jnp.where((kk > cand) if strict else (kk >= cand), 1, 0)
            for r in range(KC // SUBLANES):
                a8 = a8 + hit[r * SUBLANES:(r + 1) * SUBLANES, :]
            return a8
        a8 = lax.fori_loop(0, nkc, body, jnp.zeros((SUBLANES, QB), jnp.int32))
        return jnp.sum(a8, axis=0, keepdims=True)

    n_ties = (n_sel - count(thr, True)).astype(F32)
    has_thr = jnp.where(thr != jnp.int32(INT_MIN), 1.0, 0.0)
    ltri = jnp.where(ri >= ci, 1.0, 0.0).astype(BF16)

    def select_chunk(c, run):
        off = pl.multiple_of(c * KC, KC)
        kk = key_ref[pl.ds(off, KC), :]
        tie = jnp.where(kk == thr, has_thr, 0.0)
        rank = run + _dot(ltri, tie.astype(BF16))
        keep = jnp.where(kk > thr, 1.0, jnp.where(rank <= n_ties, tie, 0.0))
        bias_ref[pl.ds(off, KC), :] = (1.0 - keep) * NEG_BIG
        return rank[KC - 1:KC, :]

    lax.fori_loop(0, nkc, select_chunk, jnp.zeros((1, QB), F32))

    acc_ref[...] = jnp.zeros((NH, HD, QB), F32)
    scale = HD ** -0.5
    nka = lax.div(nkc + RA - 1, RA)
    for e in range(RA - 1):
        ce = nkc + e

        @pl.when(ce < nka * RA)
        def _pad(ce=ce):
            bias_ref[pl.ds(pl.multiple_of(ce * KC, KC), KC), :] = jnp.full((KC, QB), NEG_BIG, F32)

    def attn_chunk(c, carry):
        ms, ls = carry
        off = pl.multiple_of(c * KA, KA)
        bias = bias_ref[pl.ds(off, KA), :]
        new_m, new_l = [], []
        for h in range(NH):
            kh = k_ref[pl.ds(off, KA), h * HD:(h + 1) * HD]
            lg = _dot_nt(kh, aq_ref[:, h * HD:(h + 1) * HD]) * scale + bias
            m_new = jnp.maximum(ms[h], jnp.max(lg, axis=0, keepdims=True))
            alpha = jnp.exp(ms[h] - m_new)
            p = jnp.exp(lg - m_new)
            new_l.append(alpha * ls[h] + jnp.sum(p, axis=0, keepdims=True))
            acc_ref[h] = alpha * acc_ref[h] + _dot(vt_ref[c, h * HD:(h + 1) * HD, :], p.astype(BF16))
            new_m.append(m_new)
        return tuple(new_m), tuple(new_l)

    init = (tuple(jnp.full((1, QB), NEG_BIG, F32) for _ in range(NH)),
            tuple(jnp.zeros((1, QB), F32) for _ in range(NH)))
    _, ls = lax.fori_loop(0, nka, attn_chunk, init)
    for h in range(NH):
        o = acc_ref[h] / ls[h]
        out_ref[:, h * HD:(h + 1) * HD] = o.T.astype(out_ref.dtype)


def _dsa_prompt(iq, small, aq, kb, vb, *, nb, T, n_sel):
    QB = min(DSA_QB, T)
    KA = min(DSA_KA, T)
    assert T % KA == 0 and KA % QB == 0
    nq = T // QB
    aw = ATTN_HEADS * ATTN_HD
    q_map = lambda b, j: (b * nq + j, 0)
    seq_map = lambda b, j: (b, 0)
    return pl.pallas_call(
        functools.partial(_dsa_prompt_kernel, T=T, QB=QB, KA=KA, n_sel=n_sel),
        out_shape=jax.ShapeDtypeStruct((nb * T, aw), BF16),
        grid=(nb, nq),
        in_specs=[pl.BlockSpec((QB, IDX_HEADS * IDX_DIM), q_map),
                  pl.BlockSpec((QB, LANES), q_map),
                  pl.BlockSpec((T, LANES), seq_map),
                  pl.BlockSpec((QB, aw), q_map),
                  pl.BlockSpec((T, aw), seq_map),
                  pl.BlockSpec((T, aw), seq_map)],
        out_specs=pl.BlockSpec((QB, aw), q_map),
        scratch_shapes=[pltpu.VMEM((2 * T, LANES), BF16),
                        pltpu.VMEM((T // KA, aw, KA), BF16),
                        pltpu.VMEM((T, QB), jnp.int32),
                        pltpu.VMEM((T, QB), jnp.int16),
                        pltpu.VMEM((T, QB), jnp.int16),
                        pltpu.VMEM((T, QB), F32),
                        pltpu.VMEM((ATTN_HEADS, ATTN_HD, QB), F32)],
        compiler_params=_cparams(("parallel", "arbitrary")), name="dsa_prompt",
    )(iq, small, small, aq, kb, vb)


def _dsa_s_select_kernel(pt_ref, iq_ref, wcol_ref, iknew_ref, ptcol_ref, *refs, PP, n_pages, n_sel, TN, TNP):
    del pt_ref
    page_refs = refs[:PP]
    idx_ref, npast_ref, selnew_ref, key_ref = refs[PP:PP + 4]
    s = pl.program_id(1)
    ns = pl.num_programs(1)
    NPG = key_ref.shape[0]
    PS = PAGE_SIZE
    iq = iq_ref[0]
    wcol = wcol_ref[0]

    def tile_keys(d):
        x = jnp.maximum(d, 0.0) * wcol
        sc = x[0:TNP]
        for h in range(1, IDX_HEADS):
            sc = sc + x[h * TNP:(h + 1) * TNP]
        bits = pltpu.bitcast(sc, jnp.int32)
        return jnp.where(bits < 0, bits ^ jnp.int32(0x7FFFFFFF), bits)

    pages = jnp.concatenate([r[0, 0] for r in page_refs], axis=1).astype(BF16)
    dots = _dot(iq, pages)
    for pp in range(PP):
        key_ref[s * PP + pp] = tile_keys(dots[:, pp * PS:(pp + 1) * PS])

    @pl.when(s == ns - 1)
    def _select():
        kn = tile_keys(_dot(iq, iknew_ref[0].astype(BF16)))
        tq = lax.broadcasted_iota(jnp.int32, (TNP, PS), 0)
        jn = lax.broadcasted_iota(jnp.int32, (TNP, PS), 1)
        visible = jnp.logical_and(jn <= tq, jn < TN)
        key_ref[n_pages] = jnp.where(visible, kn, jnp.int32(INT_MIN))
        for p in range(n_pages + 1, NPG):
            key_ref[p] = jnp.full((TNP, PS), INT_MIN, jnp.int32)

        def count(cand, strict):
            def body(p, a):
                kk = key_ref[p]
                return a + jnp.where((kk > cand) if strict else (kk >= cand), 1, 0)
            a = lax.fori_loop(0, n_pages + 1, body, jnp.zeros((TNP, PS), jnp.int32))
            return jnp.broadcast_to(jnp.sum(a, axis=1, keepdims=True), (TNP, PS))

        zero = jnp.zeros((TNP, PS), jnp.int32)
        thr0 = jnp.where(count(zero, False) >= n_sel, zero, jnp.int32(INT_MIN))

        def bit_body(i, thr):
            cand = thr | lax.shift_left(jnp.int32(1), 30 - i)
            return jnp.where(count(cand, False) >= n_sel, cand, thr)

        thr = lax.fori_loop(0, 31, bit_body, thr0)
        n_ties = (n_sel - count(thr, True)).astype(F32)
        has_thr = jnp.where(thr != jnp.int32(INT_MIN), 1.0, 0.0)

        r_k = lax.broadcasted_iota(jnp.int32, (PS, PS), 0)
        c_k = lax.broadcasted_iota(jnp.int32, (PS, PS), 1)
        u_incl = jnp.where(r_k <= c_k, 1.0, 0.0).astype(BF16)
        u_strict = jnp.where(r_k < c_k, 1.0, 0.0).astype(BF16)
        ones_k = jnp.ones((PS, PS), BF16)
        r_g = lax.broadcasted_iota(jnp.int32, (NPG, NPG), 0)
        c_g = lax.broadcasted_iota(jnp.int32, (NPG, NPG), 1)
        l_strict = jnp.where(c_g < r_g, 1.0, 0.0).astype(BF16)
        r_p = lax.broadcasted_iota(jnp.int32, (n_pages, n_pages), 0)
        c_p = lax.broadcasted_iota(jnp.int32, (n_pages, n_pages), 1)
        l_incl = jnp.where(c_p <= r_p, 1.0, 0.0).astype(BF16)
        rep = n_sel // LANES
        slot = lax.broadcasted_iota(jnp.int32, (n_pages, n_sel), 1).astype(F32)
        jcol = lax.broadcasted_iota(jnp.int32, (PS, n_sel), 0).astype(F32)
        wide = lambda a: jnp.concatenate([a] * rep, axis=1)
        pt_w = wide(ptcol_ref[0])

        for t in range(TN):
            kt = key_ref[:, t, :]
            thr_t = thr[t:t + 1, :]
            tie = jnp.where(kt == thr_t, has_thr[t:t + 1, :], 0.0)
            tie_b = tie.astype(BF16)
            row_ties = _dot(tie_b, ones_k)
            rank = _dot(l_strict, row_ties.astype(BF16)) + _dot(tie_b, u_incl)
            sel = jnp.where(kt > thr_t, 1.0, jnp.where(rank <= n_ties[t:t + 1, :], tie, 0.0))
            selnew_ref[0, t:t + 1, :] = sel[n_pages:n_pages + 1, :]
            sp = sel[0:n_pages]
            sp_b = sp.astype(BF16)
            within = _dot(sp_b, u_strict)
            cnt = _dot(sp_b, ones_k)
            end = _dot(l_incl, cnt.astype(BF16))
            off_w, end_w = wide(end - cnt), wide(end)
            a = jnp.where(off_w <= slot, jnp.where(slot < end_w, 1.0, 0.0), 0.0)
            valid = jnp.sum(a, axis=0, keepdims=True)
            local = slot[0:1] - jnp.sum(a * off_w, axis=0, keepdims=True)
            phys = jnp.sum(a * pt_w, axis=0, keepdims=True)
            wsel = _dot_tn((sp * (within + 1.0)).astype(BF16), a.astype(BF16))
            jrow = jnp.sum(jnp.where(wsel == local + 1.0, jcol, 0.0), axis=0, keepdims=True)
            row = jnp.where(valid > 0.5, phys * PS + jrow, 0.0)
            idx_ref[0, t:t + 1, :] = row.astype(jnp.int32)
            n_row = jnp.broadcast_to(jnp.sum(valid, axis=1, keepdims=True), (1, LANES))
            npast_ref[0, t:t + 1, :] = n_row.astype(jnp.int32)
        if TNP > TN:
            idx_ref[0, TN:TNP, :] = jnp.zeros((TNP - TN, n_sel), jnp.int32)
            npast_ref[0, TN:TNP, :] = jnp.zeros((TNP - TN, LANES), jnp.int32)
            selnew_ref[0, TN:TNP, :] = jnp.zeros((TNP - TN, LANES), F32)


def _dsa_s_select(page_table, iq_hq, wcol, iknew_t, pt_col, idx_cache_t, *, layer, n_sel, TN, TNP):
    DB, n_pages = page_table.shape
    assert n_sel % LANES == 0 and PAGE_SIZE == LANES
    PP = 16
    while n_pages % PP:
        PP //= 2
    R = iq_hq.shape[1]
    NPG = -(-(n_pages + 1) // SUBLANES) * SUBLANES
    bmap = lambda b, s, pt: (b, 0, 0)
    page_specs = [pl.BlockSpec((1, 1, IDX_DIM, PAGE_SIZE), functools.partial(
        lambda b, s, pt, pp: (layer, pt[b, s * PP + pp], 0, 0), pp=pp)) for pp in range(PP)]
    grid_spec = pltpu.PrefetchScalarGridSpec(
        num_scalar_prefetch=1, grid=(DB, n_pages // PP),
        in_specs=[pl.BlockSpec((1, R, IDX_DIM), bmap),
                  pl.BlockSpec((1, R, LANES), bmap),
                  pl.BlockSpec((1, IDX_DIM, PAGE_SIZE), bmap),
                  pl.BlockSpec((1, n_pages, LANES), bmap)] + page_specs,
        out_specs=[pl.BlockSpec((1, TNP, n_sel), bmap),
                   pl.BlockSpec((1, TNP, LANES), bmap),
                   pl.BlockSpec((1, TNP, LANES), bmap)],
        scratch_shapes=[pltpu.VMEM((NPG, TNP, PAGE_SIZE), jnp.int32)])
    return pl.pallas_call(
        functools.partial(_dsa_s_select_kernel, PP=PP, n_pages=n_pages, n_sel=n_sel, TN=TN, TNP=TNP),
        out_shape=[jax.ShapeDtypeStruct((DB, TNP, n_sel), jnp.int32),
                   jax.ShapeDtypeStruct((DB, TNP, LANES), jnp.int32),
                   jax.ShapeDtypeStruct((DB, TNP, LANES), F32)],
        grid_spec=grid_spec,
        compiler_params=_cparams(("parallel", "arbitrary")), name="dsa_sample_select",
    )(page_table, iq_hq, wcol, iknew_t, pt_col, *([idx_cache_t] * PP))


def _dsa_s_attn_kernel(idx_ref, idxn_ref, npast_ref, q_ref, knew_ref, vnew_ref, selnew_ref, ck_hbm, cv_hbm,
                       out_ref, kbuf, vbuf, sem, *, layer, TN, n_sel):
    NH, HD = ATTN_HEADS, ATTN_HD
    TNP = q_ref.shape[1]
    b = pl.program_id(0)
    nb = pl.num_programs(0)
    scale = HD ** -0.5
    page_shift = PAGE_SIZE.bit_length() - 1

    def issue(rows_ref, t, slot):
        def body(i, carry):
            row = rows_ref[0, t, i]
            pg = lax.shift_right_logical(row, page_shift)
            off = row & (PAGE_SIZE - 1)
            pltpu.make_async_copy(ck_hbm.at[layer, pg, off], kbuf.at[slot, i], sem.at[0, slot]).start()
            pltpu.make_async_copy(cv_hbm.at[layer, pg, off], vbuf.at[slot, i], sem.at[1, slot]).start()
            return carry
        lax.fori_loop(0, n_sel, body, 0, unroll=8)

    def wait(slot):
        pltpu.make_async_copy(kbuf.at[slot], kbuf.at[slot], sem.at[0, slot]).wait()
        pltpu.make_async_copy(vbuf.at[slot], vbuf.at[slot], sem.at[1, slot]).wait()

    def head_tile(ref, t):
        return jnp.concatenate([ref[0, t:t + 1, h * HD:(h + 1) * HD] for h in range(NH)], axis=0).astype(F32)

    def attend(t, slot):
        qt = head_tile(q_ref, t)
        kb = kbuf[slot]
        lg = jnp.sum(kb * qt[None], axis=-1, keepdims=True) * scale
        live = lax.broadcasted_iota(jnp.int32, (n_sel, NH, 1), 0) < npast_ref[0, t, 0]
        lg = jnp.where(live, lg, NEG_BIG)
        m = jnp.max(lg, axis=0)
        new = []
        for j in range(TN):
            keep = selnew_ref[0, t:t + 1, j:j + 1] > 0.5
            lj = jnp.sum(head_tile(knew_ref, j) * qt, axis=-1, keepdims=True) * scale
            lj = jnp.where(keep, lj, NEG_BIG)
            m = jnp.maximum(m, lj)
            new.append((keep, lj))
        p = jnp.where(live, jnp.exp(lg - m[None]), 0.0)
        den = jnp.sum(p, axis=0)
        o = jnp.sum(p * vbuf[slot], axis=0)
        for j, (keep, lj) in enumerate(new):
            pj = jnp.where(keep, jnp.exp(lj - m), 0.0)
            den = den + pj
            o = o + pj * head_tile(vnew_ref, j)
        o = o / den
        for h in range(NH):
            out_ref[0, t:t + 1, h * HD:(h + 1) * HD] = o[h:h + 1, :].astype(out_ref.dtype)

    @pl.when(b == 0)
    def _first():
        issue(idx_ref, 0, 0)

    if TNP > TN:
        out_ref[0, TN:TNP, :] = jnp.zeros((TNP - TN, NH * HD), out_ref.dtype)
    for t in range(TN):
        slot = t % 2
        if t + 1 < TN:
            issue(idx_ref, t + 1, 1 - slot)
        else:
            @pl.when(b + 1 < nb)
            def _next():
                issue(idxn_ref, 0, 1 - slot)
        wait(slot)
        attend(t, slot)


def _dsa_s_attn(sel_idx, n_past, sel_new, q, knew, vnew, cache_k, cache_v, *, layer, TN):
    DB, TNP, n_sel = sel_idx.shape
    assert TN % 2 == 0
    AW = ATTN_HEADS * ATTN_HD
    bmap = lambda b: (b, 0, 0)
    nxt = lambda b: (jnp.minimum(b + 1, DB - 1), 0, 0)
    smem = pltpu.SMEM
    return pl.pallas_call(
        functools.partial(_dsa_s_attn_kernel, layer=layer, TN=TN, n_sel=n_sel),
        out_shape=jax.ShapeDtypeStruct((DB, TNP, AW), BF16),
        grid=(DB,),
        in_specs=[pl.BlockSpec((1, TNP, n_sel), bmap, memory_space=smem),
                  pl.BlockSpec((1, TNP, n_sel), nxt, memory_space=smem),
                  pl.BlockSpec((1, TNP, LANES), bmap, memory_space=smem),
                  pl.BlockSpec((1, TNP, AW), bmap),
                  pl.BlockSpec((1, TNP, AW), bmap),
                  pl.BlockSpec((1, TNP, AW), bmap),
                  pl.BlockSpec((1, TNP, LANES), bmap),
                  pl.BlockSpec(memory_space=pl.ANY),
                  pl.BlockSpec(memory_space=pl.ANY)],
        out_specs=pl.BlockSpec((1, TNP, AW), bmap),
        scratch_shapes=[pltpu.VMEM((2, n_sel, ATTN_HEADS, ATTN_HD), F32),
                        pltpu.VMEM((2, n_sel, ATTN_HEADS, ATTN_HD), F32),
                        pltpu.SemaphoreType.DMA((2, 2))],
        compiler_params=_cparams(("arbitrary",)), name="dsa_sample_attn",
    )(sel_idx, sel_idx, n_past, q, knew, vnew, sel_new, cache_k, cache_v)


def _mm_res_ln_kernel(*refs, alpha, nk, n_a, n_out):
    a_refs = refs[:n_a]
    w_ref, res_ref, g_ref, b_ref = refs[n_a:n_a + 4]
    out_refs = refs[n_a + 4:n_a + 4 + n_out]
    acc_ref = refs[n_a + 4 + n_out]
    k = pl.program_id(1)

    def finish(total):
        y = alpha * res_ref[...] + total
        mu = jnp.mean(y, axis=1, keepdims=True)
        var = jnp.mean(jnp.square(y - mu), axis=1, keepdims=True)
        out = (y - mu) * lax.rsqrt(var + LN_EPS) * g_ref[...] + b_ref[...]
        for o in out_refs:
            o[...] = out.astype(o.dtype)

    def step(first, last):
        if n_a == 1:
            part = _dot(a_refs[0][...], w_ref[...])
        else:
            kw = a_refs[0].shape[1]
            part = sum(_dot(a[...], w_ref[i * kw:(i + 1) * kw, :]) for i, a in enumerate(a_refs))
        if first and last:
            finish(part)
        elif first:
            acc_ref[...] = part
        elif last:
            finish(acc_ref[...] + part)
        else:
            acc_ref[...] += part

    if nk == 1:
        step(True, True)
    else:
        pl.when(k == 0)(functools.partial(step, True, False))
        pl.when(jnp.logical_and(k > 0, k < nk - 1))(functools.partial(step, False, False))
        pl.when(k == nk - 1)(functools.partial(step, False, True))


def _mm_res_ln(a_list, w, res, gamma, beta, *, alpha, tk=None, out_dtypes=(F32, BF16), name="mm_res_ln"):
    M = a_list[0].shape[0]
    N = w.shape[1]
    K = w.shape[0]
    tm = min(512, M)
    n_a = len(a_list)
    if n_a > 1:
        kw = a_list[0].shape[1]
        assert all(a.shape == (M, kw) for a in a_list) and kw * n_a == K
        tk = K
        a_specs = [pl.BlockSpec((tm, kw), lambda m, k: (m, 0)) for _ in a_list]
    else:
        tk = K if tk is None else tk
        assert a_list[0].shape == (M, K) and K % tk == 0
        a_specs = [pl.BlockSpec((tm, tk), lambda m, k: (m, k))]
    nk = K // tk
    assert M % tm == 0
    row = lambda m, k: (m, 0)
    return pl.pallas_call(
        functools.partial(_mm_res_ln_kernel, alpha=alpha, nk=nk, n_a=n_a, n_out=len(out_dtypes)),
        out_shape=[jax.ShapeDtypeStruct((M, N), dt) for dt in out_dtypes],
        grid=(M // tm, nk),
        in_specs=a_specs + [pl.BlockSpec((tk, N), lambda m, k: (k, 0)),
                            pl.BlockSpec((tm, N), row),
                            pl.BlockSpec((1, N), lambda m, k: (0, 0)),
                            pl.BlockSpec((1, N), lambda m, k: (0, 0))],
        out_specs=[pl.BlockSpec((tm, N), row) for _ in out_dtypes],
        scratch_shapes=[pltpu.VMEM((tm, N) if nk > 1 else (SUBLANES, LANES), F32)],
        compiler_params=_cparams(("parallel", "arbitrary")), name=name,
    )(*a_list, w, res, gamma, beta)


def _xattn_kernel(q_ref, mk_ref, mv_ref, o_ref):
    D = q_ref.shape[-1]
    hd = D // MEM_HEADS
    scale = hd ** -0.5
    for h in range(MEM_HEADS):
        q = q_ref[0, :, h * hd:(h + 1) * hd]
        kh = mk_ref[0, :, h * hd:(h + 1) * hd].astype(BF16)
        vh = mv_ref[0, :, h * hd:(h + 1) * hd].astype(BF16)
        lg = _dot_nt(q, kh) * scale
        e = jnp.exp(lg - jnp.max(lg, axis=1, keepdims=True))
        p = e / jnp.sum(e, axis=1, keepdims=True)
        o_ref[0, :, h * hd:(h + 1) * hd] = _dot(p.astype(BF16), vh).astype(o_ref.dtype)


def _xattn(q, mem_k, mem_v):
    nb, Tq, D = q.shape
    S = mem_k.shape[1]
    tq = min(512, Tq)
    assert Tq % tq == 0
    return pl.pallas_call(
        _xattn_kernel,
        out_shape=jax.ShapeDtypeStruct((nb, Tq, D), BF16),
        grid=(nb, Tq // tq),
        in_specs=[pl.BlockSpec((1, tq, D), lambda b, i: (b, i, 0)),
                  pl.BlockSpec((1, S, D), lambda b, i: (b, 0, 0)),
                  pl.BlockSpec((1, S, D), lambda b, i: (b, 0, 0))],
        out_specs=pl.BlockSpec((1, tq, D), lambda b, i: (b, i, 0)),
        compiler_params=_cparams(("parallel", "arbitrary")), name="xattn",
    )(q, mem_k, mem_v)


def _ffn_act_kernel(x_ref, wg_ref, wu_ref, o_ref):
    x = x_ref[...]
    gte = _dot(x, wg_ref[...])
    up = _dot(x, wu_ref[...])
    o_ref[...] = (gte * jax.nn.sigmoid(gte) * up).astype(o_ref.dtype)


def _ffn_tile(n):
    for t in (512, 256, 128):
        if n % t == 0:
            return t
    return n


def _ffn_act(x, wg, wu):
    M, K = x.shape
    N = wg.shape[1]
    tm = min(1024, M)
    tn = _ffn_tile(N)
    assert M % tm == 0
    return pl.pallas_call(
        _ffn_act_kernel,
        out_shape=jax.ShapeDtypeStruct((M, N), BF16),
        grid=(N // tn, M // tm),
        in_specs=[pl.BlockSpec((tm, K), lambda n, m: (m, 0)),
                  pl.BlockSpec((K, tn), lambda n, m: (0, n)),
                  pl.BlockSpec((K, tn), lambda n, m: (0, n))],
        out_specs=pl.BlockSpec((tm, tn), lambda n, m: (m, n)),
        compiler_params=_cparams(("parallel", "parallel")), name="ffn_act",
    )(x, wg, wu)


def _in_weights(w_in):
    H, DK, DV = MLSTM_HEADS, MLSTM_QK, MLSTM_V
    aw = ATTN_HEADS * ATTN_HD
    iw = IDX_HEADS * IDX_DIM
    o = 0
    w_qkv = w_in[:, o:o + 2 * H * DK + H * DV]; o += 2 * H * DK + H * DV
    w_mo = w_in[:, o:o + H * DV]; o += H * DV
    w_ig = w_in[:, o:o + H]; o += H
    w_fg = w_in[:, o:o + H]; o += H
    w_aq = w_in[:, o:o + aw]; o += aw
    w_ak = w_in[:, o:o + aw]; o += aw
    w_av = w_in[:, o:o + aw]; o += aw
    w_iq = w_in[:, o:o + iw]; o += iw
    w_ik = w_in[:, o:o + IDX_DIM]; o += IDX_DIM
    w_iw = w_in[:, o:o + IDX_HEADS]; o += IDX_HEADS
    assert o == w_in.shape[1]
    pad = jnp.zeros((w_in.shape[0], LANES - IDX_DIM - 2 * H - IDX_HEADS), w_in.dtype)
    w_small = jnp.concatenate([w_ik, w_ig, w_fg, w_iw, pad], axis=1)
    return tuple(w.astype(BF16) for w in (w_qkv, w_mo, w_aq, w_ak, w_av, w_iq, w_small))


def _mixer_inputs(xb, w, bias_small, pos, tag):
    w_qkv, w_mo, w_aq, w_ak, w_av, w_iq, w_small = w
    cos_a, sin_a = _rope_tables(pos, ATTN_HD)
    cos_i, sin_i = _rope_tables(pos, IDX_DIM)
    cos_s, sin_s = _small_tables(pos)
    (qkv,) = _proj(xb, w_qkv, [BF16], name=tag + "_proj_qkv")
    (mo,) = _proj(xb, w_mo, [F32], name=tag + "_proj_mo")
    (aq,) = _proj(xb, w_aq, [BF16], rope=1, cos=cos_a, sin=sin_a, name=tag + "_proj_aq")
    ak, akb = _proj(xb, w_ak, [F32, BF16], rope=1, cos=cos_a, sin=sin_a, name=tag + "_proj_ak")
    av, avb = _proj(xb, w_av, [F32, BF16], name=tag + "_proj_av")
    (iq,) = _proj(xb, w_iq, [BF16], rope=2, cos=cos_i, sin=sin_i, name=tag + "_proj_iq")
    (small,) = _proj(xb, w_small, [F32], rope=2, cos=cos_s, sin=sin_s, bias=bias_small,
                     name=tag + "_proj_small")
    return qkv, mo, aq, ak, akb, av, avb, iq, small


def _block_tail(x2d, heads, mem_k, mem_v, nb, w_out, ln1, w_cq, w_co, ln2, w_gate, w_up, w_down, ln3,
                alpha, tag):
    M, D = x2d.shape
    x1, x1b = _mm_res_ln(heads, w_out, x2d, ln1[0], ln1[1], alpha=alpha, name=tag + "_out_ln1")
    (cq,) = _proj(x1b, w_cq, [BF16], name=tag + "_proj_cq")
    ctx = _xattn(cq.reshape(nb, M // nb, D), mem_k, mem_v).reshape(M, D)
    x2, x2b = _mm_res_ln([ctx], w_co, x1, ln2[0], ln2[1], alpha=alpha, name=tag + "_co_ln2")
    act = _ffn_act(x2b, w_gate, w_up)
    F = act.shape[1]
    tk = F
    for cand in (1408, 1024, 512):
        if F % cand == 0 and F > cand:
            tk = cand
            break
    (x3,) = _mm_res_ln([act], w_down, x2, ln3[0], ln3[1], alpha=alpha, tk=tk, out_dtypes=(F32,),
                       name=tag + "_down_ln3")
    return x3


def kernel(x_prompt, x_sample, cache_k, cache_v, cache_idx_k, state_C, state_n, state_m, cache_mem_k, cache_mem_v, page_table, mem_prompt, w_in, b_igate, b_fgate, mlstm_norm_g, w_out, ln1_g, ln1_b, w_cq, w_ck, w_cv, w_co, ln2_g, ln2_b, w_gate, w_up, w_down, ln3_g, ln3_b):
    B, T, D = x_prompt.shape
    DB, TN, _ = x_sample.shape
    depth = w_in.shape[0]
    n_pages = page_table.shape[1]
    past = n_pages * PAGE_SIZE
    n_mem = mem_prompt.shape[1]
    H, DK, DV = MLSTM_HEADS, MLSTM_QK, MLSTM_V
    AW = ATTN_HEADS * ATTN_HD
    MIX = H * DV + AW
    TNP = -(-TN // SUBLANES) * SUBLANES
    assert TNP == SUBLANES
    alpha = (2 * depth) ** 0.25
    n_sel_p = min(TOPK_MAX, T // 4)
    n_sel_s = min(TOPK_MAX, (past + TN) // 4)
    L = MLSTM_CHUNK if T % MLSTM_CHUNK == 0 else T
    page_table = page_table.astype(jnp.int32)
    idx_cache_t = jnp.swapaxes(cache_idx_k, 2, 3)

    pos_p = jnp.arange(T)
    pos_s = past + (jnp.arange(DB * TNP) % TNP)

    xp = x_prompt.reshape(B * T, D)
    xs = jnp.pad(x_sample, ((0, 0), (0, TNP - TN), (0, 0))).reshape(DB * TNP, D)
    new = [[] for _ in range(14)]
    row2 = lambda a: a.reshape(1, -1).astype(F32)

    for l in range(depth):
        w_l = _in_weights(w_in[l])
        zpad = jnp.zeros((LANES - SM_IW,), F32)
        bias_small = jnp.concatenate([jnp.zeros((SM_IG,), F32), b_igate[l].astype(F32),
                                      b_fgate[l].astype(F32), zpad]).reshape(1, LANES)
        norm_g = row2(mlstm_norm_g[l])
        w_out_b, w_cq_b, w_co_b = (w.astype(BF16) for w in (w_out[l], w_cq[l], w_co[l]))
        w_gate_b, w_up_b, w_down_b = (w.astype(BF16) for w in (w_gate[l], w_up[l], w_down[l]))
        ln1 = (row2(ln1_g[l]), row2(ln1_b[l]))
        ln2 = (row2(ln2_g[l]), row2(ln2_b[l]))
        ln3 = (row2(ln3_g[l]), row2(ln3_b[l]))
        tail_w = (w_out_b, ln1, w_cq_b, w_co_b, ln2, w_gate_b, w_up_b, w_down_b, ln3)

        qkv, mo, aq, ak, akb, av, avb, iq, small = _mixer_inputs(xp.astype(BF16), w_l, bias_small, pos_p, "p")
        hm, pC, pn, pm = _mlstm(
            qkv, small, mo, norm_g, jnp.zeros((B, H, DV, DK), F32), jnp.zeros((B, H, DK), F32),
            jnp.zeros((B, H), F32), nb=B, T=T, L=L, valid=L)
        ha = _dsa_prompt(iq, small, aq, akb, avb, nb=B, T=T, n_sel=n_sel_p)
        memb = mem_prompt.reshape(B * n_mem, D).astype(BF16)
        (mem_k,) = _proj(memb, w_ck[l].astype(BF16), [F32], name="p_proj_ck")
        (mem_v,) = _proj(memb, w_cv[l].astype(BF16), [F32], name="p_proj_cv")
        mem_k = mem_k.reshape(B, n_mem, D)
        mem_v = mem_v.reshape(B, n_mem, D)
        xp = _block_tail(xp, [hm, ha], mem_k, mem_v, B, *tail_w, alpha, "p")

        sqkv, smo, saq, sak, sakb, sav, savb, siq, ssmall = _mixer_inputs(
            xs.astype(BF16), w_l, bias_small, pos_s, "s")
        shm, sC, sn, sm = _mlstm(sqkv, ssmall, smo, norm_g, state_C[l].astype(F32), state_n[l].astype(F32),
                                 state_m[l].astype(F32), nb=DB, T=TNP, L=TNP, valid=TN)
        iq_hq = siq.reshape(DB, TNP, IDX_HEADS, IDX_DIM).transpose(0, 2, 1, 3).reshape(DB, IDX_HEADS * TNP, IDX_DIM)
        w_hq = ssmall[:, SM_IW:SM_IW + IDX_HEADS].reshape(DB, TNP, IDX_HEADS).transpose(0, 2, 1)
        w_hq = w_hq.reshape(DB, IDX_HEADS * TNP, 1) * (IDX_HEADS * IDX_DIM) ** -0.5
        wcol = jnp.broadcast_to(w_hq, (DB, IDX_HEADS * TNP, LANES))
        ik_new_t = jnp.pad(ssmall[:, SM_IK:SM_IK + IDX_DIM].reshape(DB, TNP, IDX_DIM),
                           ((0, 0), (0, PAGE_SIZE - TNP), (0, 0))).swapaxes(1, 2)
        pt_col = jnp.broadcast_to(page_table.astype(F32)[:, :, None], (DB, n_pages, LANES))
        sel_idx, n_past, sel_new = _dsa_s_select(page_table, iq_hq, wcol, ik_new_t, pt_col, idx_cache_t,
                                                 layer=l, n_sel=n_sel_s, TN=TN, TNP=TNP)
        sha = _dsa_s_attn(sel_idx, n_past, sel_new, saq.reshape(DB, TNP, AW), sakb.reshape(DB, TNP, AW),
                          savb.reshape(DB, TNP, AW), cache_k, cache_v, layer=l, TN=TN).reshape(DB * TNP, AW)
        xs = _block_tail(xs, [shm, sha], cache_mem_k[l].reshape(DB, n_mem, D), cache_mem_v[l].reshape(DB, n_mem, D),
                         DB, *tail_w, alpha, "s")

        def s_rows(a, shape):
            return a.reshape(DB, TNP, -1)[:, :TN].reshape(shape)

        vals = (ak.reshape(B, T, ATTN_HEADS, ATTN_HD), av.reshape(B, T, ATTN_HEADS, ATTN_HD),
                small[:, SM_IK:SM_IK + IDX_DIM].reshape(B, T, IDX_DIM), pC, pn, pm,
                mem_k.reshape(B, n_mem, MEM_HEADS, D // MEM_HEADS), mem_v.reshape(B, n_mem, MEM_HEADS, D // MEM_HEADS),
                s_rows(sak, (DB, TN, ATTN_HEADS, ATTN_HD)), s_rows(sav, (DB, TN, ATTN_HEADS, ATTN_HD)),
                s_rows(ssmall[:, SM_IK:SM_IK + IDX_DIM], (DB, TN, IDX_DIM)), sC, sn, sm)
        for slot, val in zip(new, vals):
            slot.append(val)

    outs = [jnp.stack(slot) for slot in new]
    y_prompt = xp.reshape(B, T, D)
    y_sample = xs.reshape(DB, TNP, D)[:, :TN]
    return (y_prompt, y_sample, *outs)
```

```python
import functools

import jax
import jax.numpy as jnp
from jax import lax
from jax.experimental import pallas as pl
from jax.experimental.pallas import tpu as pltpu

MLSTM_HEADS = 4
MLSTM_QK = 128
MLSTM_V = 256
ATTN_HEADS = 8
ATTN_HD = 128
IDX_HEADS = 16
IDX_DIM = 64
TOPK_MAX = 256
PAGE_SIZE = 128
MEM_HEADS = 4
ROPE_THETA = 10000.0
LN_EPS = 1e-5

LANES = 128
SUBLANES = 8
VMEM_LIMIT_BYTES = 52 * 1024 * 1024

SM_IK = 0
SM_IG = IDX_DIM
SM_FG = IDX_DIM + MLSTM_HEADS
SM_IW = IDX_DIM + 2 * MLSTM_HEADS

MLSTM_CHUNK = 256
DSA_QB = 256
DSA_KA = 512
MLSTM_EXT = MLSTM_V + LANES
NEG_BIG = -1e30
INT_MIN = -(2 ** 31)

F32 = jnp.float32
BF16 = jnp.bfloat16


def _cparams(sem):
    return pltpu.CompilerParams(dimension_semantics=sem, vmem_limit_bytes=VMEM_LIMIT_BYTES)


def _dot(a, b):
    return jnp.dot(a, b, preferred_element_type=F32)


def _dot_nt(a, b):
    return lax.dot_general(a, b, (((1,), (1,)), ((), ())), preferred_element_type=F32)


def _dot_tn(a, b):
    return lax.dot_general(a, b, (((0,), (0,)), ((), ())), preferred_element_type=F32)


def _split3(x):
    p0 = x.astype(BF16)
    r = x - p0.astype(F32)
    p1 = r.astype(BF16)
    r = r - p1.astype(F32)
    return p0, p1, r.astype(BF16)


def _dot_exact_rhs(a_b16, x_f32):
    p0, p1, p2 = _split3(x_f32)
    return _dot(a_b16, p0) + _dot(a_b16, p1) + _dot(a_b16, p2)


def _proj_kernel(*refs, rope, has_bias, n_out, tn):
    x_ref, w_ref = refs[0], refs[1]
    pos = 2
    cos_ref = sin_ref = bias_ref = None
    if rope:
        cos_ref, sin_ref = refs[pos], refs[pos + 1]
        pos += 2
    if has_bias:
        bias_ref = refs[pos]
        pos += 1
    out_refs = refs[pos:pos + n_out]
    acc = _dot(x_ref[...], w_ref[...])
    tm = acc.shape[0]
    if rope:
        cos = cos_ref[...]
        sin = sin_ref[...]
        lane = lax.broadcasted_iota(jnp.int32, (tm, LANES), 1)
    for j in range(tn // LANES):
        blk = acc[:, j * LANES:(j + 1) * LANES]
        if rope == 1:
            partner = pltpu.roll(blk, 64, 1)
            blk = blk * cos + partner * sin
        elif rope == 2:
            partner = jnp.where((lane % 64) < 32, pltpu.roll(blk, 96, 1), pltpu.roll(blk, 32, 1))
            blk = blk * cos + partner * sin
        if has_bias:
            blk = blk + bias_ref[...]
        for o in out_refs:
            o[:, j * LANES:(j + 1) * LANES] = blk.astype(o.dtype)


def _proj(x, w, out_dtypes, rope=0, cos=None, sin=None, bias=None, name="proj"):
    M, K = x.shape
    N = w.shape[1]
    tm = min(1024, M)
    tn = min(1024, N)
    assert M % tm == 0 and N % tn == 0 and tn % LANES == 0
    grid = (N // tn, M // tm)
    in_specs = [pl.BlockSpec((tm, K), lambda n, m: (m, 0)),
                pl.BlockSpec((K, tn), lambda n, m: (0, n))]
    args = [x, w]
    if rope:
        if cos.shape[0] < tm:
            assert tm % cos.shape[0] == 0
            cos = jnp.tile(cos, (tm // cos.shape[0], 1))
            sin = jnp.tile(sin, (tm // sin.shape[0], 1))
        rows = cos.shape[0]
        assert rows % tm == 0
        nrb = rows // tm
        tab_spec = pl.BlockSpec((tm, LANES), lambda n, m: (m % nrb, 0))
        in_specs += [tab_spec, tab_spec]
        args += [cos, sin]
    if bias is not None:
        assert tn == LANES
        in_specs.append(pl.BlockSpec((1, LANES), lambda n, m: (0, 0)))
        args.append(bias)
    out_shape = [jax.ShapeDtypeStruct((M, N), dt) for dt in out_dtypes]
    out_specs = [pl.BlockSpec((tm, tn), lambda n, m: (m, n)) for _ in out_dtypes]
    outs = pl.pallas_call(
        functools.partial(_proj_kernel, rope=rope, has_bias=bias is not None,
                          n_out=len(out_dtypes), tn=tn),
        out_shape=out_shape, grid=grid, in_specs=in_specs, out_specs=out_specs,
        compiler_params=_cparams(("parallel", "parallel")), name=name,
    )(*args)
    return outs


def _rope_tables(pos, head_dim):
    half = head_dim // 2
    inv_freq = ROPE_THETA ** (-jnp.arange(half, dtype=F32) / half)
    ang = pos.astype(F32)[:, None] * inv_freq
    cos, sin = jnp.cos(ang), jnp.sin(ang)
    reps = LANES // head_dim
    cos_full = jnp.tile(jnp.concatenate([cos, cos], axis=-1), (1, reps))
    sin_full = jnp.tile(jnp.concatenate([-sin, sin], axis=-1), (1, reps))
    return cos_full, sin_full


def _small_tables(pos):
    cos_i, sin_i = _rope_tables(pos, IDX_DIM)
    lane = jnp.arange(LANES)[None, :]
    return jnp.where(lane < IDX_DIM, cos_i, 1.0), jnp.where(lane < IDX_DIM, sin_i, 0.0)


def _mlstm_kernel(qkv_ref, g_ref, mo_ref, ng_ref, c0_ref, n0_ref, m0_ref,
                  hm_ref, cout_ref, nout_ref, mout_ref, cext_ref, m_ref, *, L, valid):
    H, DK, DV = MLSTM_HEADS, MLSTM_QK, MLSTM_V
    c = pl.program_id(1)
    nc = pl.num_programs(1)
    row8 = lax.broadcasted_iota(jnp.int32, (SUBLANES, DK), 0)

    @pl.when(c == 0)
    def _init():
        for h in range(H):
            cext_ref[h, 0:DV, :] = c0_ref[0, h]
            n_row = jnp.broadcast_to(n0_ref[0, h:h + 1, :], (SUBLANES, DK))
            cext_ref[h, DV:DV + SUBLANES, :] = jnp.where(row8 == 0, n_row, 0.0)
            cext_ref[h, DV + SUBLANES:, :] = jnp.zeros((MLSTM_EXT - DV - SUBLANES, DK), F32)
            m_ref[h] = jnp.broadcast_to(m0_ref[0, h:h + 1, :], (SUBLANES, LANES))

    scale = DK ** -0.5
    g = g_ref[...]
    row = lax.broadcasted_iota(jnp.int32, (L, 1), 0)
    ri = lax.broadcasted_iota(jnp.int32, (L, L), 0)
    ci = lax.broadcasted_iota(jnp.int32, (L, L), 1)
    causal = ci <= ri
    lt_b = jnp.where(causal, 1.0, 0.0).astype(BF16)
    ones_b = jnp.ones((L, L), BF16)
    lane_e = lax.broadcasted_iota(jnp.int32, (L, LANES), 1)
    one_col = jnp.where(lane_e == 0, 1.0, 0.0).astype(BF16)

    for h in range(H):
        q = qkv_ref[:, h * DK:(h + 1) * DK]
        k = qkv_ref[:, H * DK + h * DK:H * DK + (h + 1) * DK]
        v = qkv_ref[:, 2 * H * DK + h * DV:2 * H * DK + (h + 1) * DV]
        i_col = g[:, SM_IG + h:SM_IG + h + 1]
        lf_col = jax.nn.log_sigmoid(g[:, SM_FG + h:SM_FG + h + 1])
        if valid < L:
            i_col = jnp.where(row < valid, i_col, NEG_BIG)
            lf_col = jnp.where(row < valid, lf_col, 0.0)
        rhs = jnp.concatenate([jnp.where(ri > ci, lf_col, 0.0),
                               jnp.broadcast_to(lf_col, (L, LANES))], axis=1)
        cs = _dot_exact_rhs(lt_b, rhs)
        dsum = cs[:, :L]
        bcol = cs[:, L:L + 1]
        imat = _dot_exact_rhs(ones_b, jnp.where(ri == ci, i_col, 0.0))
        m_prev = m_ref[h, 0:1, 0:1]
        inter = bcol + m_prev
        dlog = jnp.where(causal, dsum + imat, -jnp.inf)
        m_t = jnp.maximum(inter, jnp.max(dlog, axis=1, keepdims=True))
        w_prev = jnp.exp(inter - m_t)
        s_mat = (_dot_nt(q, k) * scale) * jnp.exp(dlog - m_t)
        vext = jnp.concatenate([v, one_col], axis=1)
        cb = cext_ref[h].astype(BF16)
        numext = w_prev * _dot_nt(q, cb) + _dot(s_mat.astype(BF16), vext)
        num = numext[:, :DV]
        nq = numext[:, DV:DV + 1]
        hh = num / jnp.maximum(jnp.abs(nq), jnp.exp(-m_t))
        mu = jnp.mean(hh, axis=1, keepdims=True)
        var = jnp.mean(jnp.square(hh - mu), axis=1, keepdims=True)
        hn = (hh - mu) * lax.rsqrt(var + LN_EPS)
        gate = jax.nn.sigmoid(mo_ref[:, h * DV:(h + 1) * DV])
        hm_ref[:, h * DV:(h + 1) * DV] = (hn * ng_ref[:, h * DV:(h + 1) * DV] * gate).astype(hm_ref.dtype)
        m_new = m_t[L - 1:L, :]
        b_last = bcol[L - 1:L, :]
        decay = jnp.exp(b_last + m_prev - m_new)
        ws_col = jnp.exp(b_last - bcol + i_col - m_new) * scale
        xw = (ws_col * vext.astype(F32)).astype(BF16)
        cext_ref[h] = decay * cext_ref[h] + _dot_tn(xw, k)
        m_ref[h] = jnp.broadcast_to(m_new, (SUBLANES, LANES))

    @pl.when(c == nc - 1)
    def _fin():
        for h in range(H):
            cout_ref[0, h] = cext_ref[h, 0:DV, :]
            nout_ref[0, h:h + 1, :] = cext_ref[h, DV:DV + 1, :]
            mout_ref[0, h:h + 1, :] = m_ref[h, 0:1, :]


def _mlstm(qkv, small, mo, norm_g, c0, n0, m0, *, nb, T, L, valid):
    H, DK, DV = MLSTM_HEADS, MLSTM_QK, MLSTM_V
    nc = T // L
    hw = H * DV
    m0r = jnp.broadcast_to(m0[:, :, None], (nb, H, LANES)).astype(F32)
    row_map = lambda b, c: (b * nc + c, 0)
    st4 = lambda b, c: (b, 0, 0, 0)
    st3 = lambda b, c: (b, 0, 0)
    outs = pl.pallas_call(
        functools.partial(_mlstm_kernel, L=L, valid=valid),
        out_shape=[jax.ShapeDtypeStruct((nb * T, hw), BF16),
                   jax.ShapeDtypeStruct((nb, H, DV, DK), F32),
                   jax.ShapeDtypeStruct((nb, H, DK), F32),
                   jax.ShapeDtypeStruct((nb, H, LANES), F32)],
        grid=(nb, nc),
        in_specs=[pl.BlockSpec((L, qkv.shape[1]), row_map),
                  pl.BlockSpec((L, LANES), row_map),
                  pl.BlockSpec((L, hw), row_map),
                  pl.BlockSpec((1, hw), lambda b, c: (0, 0)),
                  pl.BlockSpec((1, H, DV, DK), st4),
                  pl.BlockSpec((1, H, DK), st3),
                  pl.BlockSpec((1, H, LANES), st3)],
        out_specs=[pl.BlockSpec((L, hw), row_map),
                   pl.BlockSpec((1, H, DV, DK), st4),
                   pl.BlockSpec((1, H, DK), st3),
                   pl.BlockSpec((1, H, LANES), st3)],
        scratch_shapes=[pltpu.VMEM((H, MLSTM_EXT, DK), F32), pltpu.VMEM((H, SUBLANES, LANES), F32)],
        compiler_params=_cparams(("parallel", "arbitrary")), name="mlstm",
    )(qkv, small, mo, norm_g, c0, n0, m0r)
    hm, c_out, n_out, m_out = outs
    return hm, c_out, n_out, m_out[:, :, 0]


def _dsa_prompt_kernel(iq_ref, smq_ref, smk_ref, aq_ref, k_ref, v_ref, out_ref,
                       ikx_ref, vt_ref, key_ref, hi_ref, lo_ref, bias_ref, acc_ref, lg_ref, p_ref,
                       *, T, QB, KA, n_sel):
    KC = QB
    RA = KA // KC
    NH, HD = ATTN_HEADS, ATTN_HD
    j = pl.program_id(1)
    nkc = j + 1

    @pl.when(j == 0)
    def _per_sequence():
        sm = smk_ref[...]
        lane = lax.broadcasted_iota(jnp.int32, (T, LANES), 1)
        ikx_ref[0:T, :] = jnp.where(lane < IDX_DIM, sm, 0.0).astype(BF16)
        ikx_ref[T:2 * T, :] = jnp.where(lane >= IDX_DIM, pltpu.roll(sm, IDX_DIM, 1), 0.0).astype(BF16)
        for c in range(T // KA):
            vt_ref[c] = v_ref[c * KA:(c + 1) * KA, :].astype(F32).T.astype(BF16)

    w_t = smq_ref[...].T * (IDX_HEADS * IDX_DIM) ** -0.5
    ri = lax.broadcasted_iota(jnp.int32, (KC, QB), 0)
    ci = lax.broadcasted_iota(jnp.int32, (KC, QB), 1)

    def score_chunk(c, carry):
        off = pl.multiple_of(c * KC, KC)
        acc = jnp.zeros((KC, QB), F32)
        for p in range(IDX_HEADS // 2):
            rhs = iq_ref[:, p * LANES:(p + 1) * LANES]
            d_even = _dot_nt(ikx_ref[pl.ds(off, KC), :], rhs)
            d_odd = _dot_nt(ikx_ref[pl.ds(T + off, KC), :], rhs)
            acc = acc + w_t[SM_IW + 2 * p:SM_IW + 2 * p + 1, :] * jnp.maximum(d_even, 0.0)
            acc = acc + w_t[SM_IW + 2 * p + 1:SM_IW + 2 * p + 2, :] * jnp.maximum(d_odd, 0.0)
        bits = pltpu.bitcast(acc, jnp.int32)
        key = jnp.where(bits < 0, bits ^ jnp.int32(0x7FFFFFFF), bits)
        visible = jnp.logical_or(ri <= ci, c < j)
        key = jnp.where(visible, key, jnp.int32(INT_MIN))
        key_ref[pl.ds(off, KC), :] = key
        hi_ref[pl.ds(off, KC), :] = lax.shift_right_arithmetic(key, 16).astype(jnp.int16)
        return carry

    lax.fori_loop(0, nkc, score_chunk, 0)

    HALF_MIN = -(2 ** 15)
    PK = 2 * SUBLANES

    def count_half(ref, cand, strict=False):
        c16 = jnp.broadcast_to(cand, (PK, QB)).astype(jnp.int16)
        one = jnp.ones((PK, QB), jnp.int16)
        zero16 = jnp.zeros((PK, QB), jnp.int16)

        def body(c, a):
            off = pl.multiple_of(c * KC, KC)
            kk = ref[pl.ds(off, KC), :]
            for r in range(KC // PK):
                blk = kk[r * PK:(r + 1) * PK, :]
                a = a + jnp.where((blk > c16) if strict else (blk >= c16), one, zero16)
            return a
        a = lax.fori_loop(0, nkc, body, zero16)
        return jnp.sum(a.astype(jnp.int32), axis=0, keepdims=True)

    def search_half(ref, need):
        zero = jnp.zeros((1, QB), jnp.int32)
        t0 = jnp.where(count_half(ref, zero) >= need, zero, jnp.int32(HALF_MIN))

        def bit_body(i, t):
            cand = t | lax.shift_left(jnp.int32(1), 14 - i)
            return jnp.where(count_half(ref, cand) >= need, cand, t)
        return lax.fori_loop(0, 15, bit_body, t0)

    thr_hi = search_half(hi_ref, n_sel)
    need_lo = n_sel - count_half(hi_ref, thr_hi, strict=True)

    def low_chunk(c, carry):
        off = pl.multiple_of(c * KC, KC)
        kk = key_ref[pl.ds(off, KC), :]
        lo = (kk & jnp.int32(0xFFFF)) + HALF_MIN
        same = lax.shift_right_arithmetic(kk, 16) == thr_hi
        lo_ref[pl.ds(off, KC), :] = jnp.where(same, lo, jnp.int32(HALF_MIN)).astype(jnp.int16)
        return carry

    lax.fori_loop(0, nkc, low_chunk, 0)
    thr = thr_hi * 65536 + (search_half(lo_ref, need_lo) - HALF_MIN)

    def count(cand, strict):
        def body(c, a8):
            off = pl.multiple_of(c * KC, KC)
            kk = key_ref[pl.ds(off, KC), :]
            hit = jnp.where((kk > cand) if strict else (kk >= cand), 1, 0)
            for r in range(KC // SUBLANES):
                a8 = a8 + hit[r * SUBLANES:(r + 1) * SUBLANES, :]
            return a8
        a8 = lax.fori_loop(0, nkc, body, jnp.zeros((SUBLANES, QB), jnp.int32))
        return jnp.sum(a8, axis=0, keepdims=True)

    n_ties = (n_sel - count(thr, True)).astype(F32)
    has_thr = jnp.where(thr != jnp.int32(INT_MIN), 1.0, 0.0)
    ltri = jnp.where(ri >= ci, 1.0, 0.0).astype(BF16)

    def select_chunk(c, run):
        off = pl.multiple_of(c * KC, KC)
        kk = key_ref[pl.ds(off, KC), :]
        tie = jnp.where(kk == thr, has_thr, 0.0)
        rank = run + _dot(ltri, tie.astype(BF16))
        keep = jnp.where(kk > thr, 1.0, jnp.where(rank <= n_ties, tie, 0.0))
        bias_ref[pl.ds(off, KC), :] = (1.0 - keep) * NEG_BIG
        return rank[KC - 1:KC, :]

    lax.fori_loop(0, nkc, select_chunk, jnp.zeros((1, QB), F32))

    acc_ref[...] = jnp.zeros((NH, HD, QB), F32)
    scale = HD ** -0.5
    nka = lax.div(nkc + RA - 1, RA)
    for e in range(RA - 1):
        ce = nkc + e

        @pl.when(ce < nka * RA)
        def _pad(ce=ce):
            bias_ref[pl.ds(pl.multiple_of(ce * KC, KC), KC), :] = jnp.full((KC, QB), NEG_BIG, F32)

    def attn_chunk(c, carry):
        ms, ls = carry
        off = pl.multiple_of(c * KA, KA)
        for h in range(NH):
            kh = k_ref[pl.ds(off, KA), h * HD:(h + 1) * HD]
            lg_ref[h] = _dot_nt(kh, aq_ref[:, h * HD:(h + 1) * HD])
        bias = bias_ref[pl.ds(off, KA), :]
        new_m, new_l = [], []
        for h in range(NH):
            lg = lg_ref[h] * scale + bias
            m_new = jnp.maximum(ms[h], jnp.max(lg, axis=0, keepdims=True))
            alpha = jnp.exp(ms[h] - m_new)
            p = jnp.exp(lg - m_new)
            new_l.append(alpha * ls[h] + jnp.sum(p, axis=0, keepdims=True))
            p_ref[h] = p.astype(BF16)
            acc_ref[h] = alpha * acc_ref[h]
            new_m.append(m_new)
        for h in range(NH):
            acc_ref[h] += _dot(vt_ref[c, h * HD:(h + 1) * HD, :], p_ref[h])
        return tuple(new_m), tuple(new_l)

    init = (tuple(jnp.full((1, QB), NEG_BIG, F32) for _ in range(NH)),
            tuple(jnp.zeros((1, QB), F32) for _ in range(NH)))
    _, ls = lax.fori_loop(0, nka, attn_chunk, init)
    for h in range(NH):
        o = acc_ref[h] / ls[h]
        out_ref[:, h * HD:(h + 1) * HD] = o.T.astype(out_ref.dtype)


def _dsa_prompt(iq, small, aq, kb, vb, *, nb, T, n_sel):
    QB = min(DSA_QB, T)
    KA = min(DSA_KA, T)
    assert T % KA == 0 and KA % QB == 0
    nq = T // QB
    aw = ATTN_HEADS * ATTN_HD
    q_map = lambda b, j: (b * nq + j, 0)
    seq_map = lambda b, j: (b, 0)
    return pl.pallas_call(
        functools.partial(_dsa_prompt_kernel, T=T, QB=QB, KA=KA, n_sel=n_sel),
        out_shape=jax.ShapeDtypeStruct((nb * T, aw), BF16),
        grid=(nb, nq),
        in_specs=[pl.BlockSpec((QB, IDX_HEADS * IDX_DIM), q_map),
                  pl.BlockSpec((QB, LANES), q_map),
                  pl.BlockSpec((T, LANES), seq_map),
                  pl.BlockSpec((QB, aw), q_map),
                  pl.BlockSpec((T, aw), seq_map),
                  pl.BlockSpec((T, aw), seq_map)],
        out_specs=pl.BlockSpec((QB, aw), q_map),
        scratch_shapes=[pltpu.VMEM((2 * T, LANES), BF16),
                        pltpu.VMEM((T // KA, aw, KA), BF16),
                        pltpu.VMEM((T, QB), jnp.int32),
                        pltpu.VMEM((T, QB), jnp.int16),
                        pltpu.VMEM((T, QB), jnp.int16),
                        pltpu.VMEM((T, QB), F32),
                        pltpu.VMEM((ATTN_HEADS, ATTN_HD, QB), F32),
                        pltpu.VMEM((ATTN_HEADS, KA, QB), F32),
                        pltpu.VMEM((ATTN_HEADS, KA, QB), BF16)],
        compiler_params=_cparams(("parallel", "arbitrary")), name="dsa_prompt",
    )(iq, small, small, aq, kb, vb)


def _dsa_s_select_kernel(pt_ref, iq_ref, wcol_ref, iknew_ref, ptcol_ref, cache_hbm,
                         idx_ref, npast_ref, selnew_ref, key_ref, pbuf, sem, *, layer, PP, n_pages, n_sel, TN, TNP):
    b = pl.program_id(0)
    nb = pl.num_programs(0)
    slot = lax.rem(b, 2)
    NPG = key_ref.shape[0]
    PS = PAGE_SIZE
    iq = iq_ref[0]
    wcol = wcol_ref[0]

    def tile_keys(d):
        x = jnp.maximum(d, 0.0) * wcol
        sc = x[0:TNP]
        for h in range(1, IDX_HEADS):
            sc = sc + x[h * TNP:(h + 1) * TNP]
        bits = pltpu.bitcast(sc, jnp.int32)
        return jnp.where(bits < 0, bits ^ jnp.int32(0x7FFFFFFF), bits)

    def fetch(seq, buf):
        def body(p, carry):
            pltpu.make_async_copy(cache_hbm.at[layer, pt_ref[seq, p]], pbuf.at[buf, p], sem.at[buf]).start()
            return carry
        lax.fori_loop(0, n_pages, body, 0, unroll=8)

    @pl.when(b == 0)
    def _first():
        fetch(0, 0)

    @pl.when(b + 1 < nb)
    def _next():
        fetch(b + 1, 1 - slot)

    pltpu.make_async_copy(pbuf.at[slot], pbuf.at[slot], sem.at[slot]).wait()

    def score_group(g, carry):
        pages = jnp.concatenate([pbuf[slot, g * PP + pp] for pp in range(PP)], axis=1).astype(BF16)
        dots = _dot(iq, pages)
        for pp in range(PP):
            key_ref[g * PP + pp] = tile_keys(dots[:, pp * PS:(pp + 1) * PS])
        return carry

    lax.fori_loop(0, n_pages // PP, score_group, 0)

    def select():
        kn = tile_keys(_dot(iq, iknew_ref[0].astype(BF16)))
        tq = lax.broadcasted_iota(jnp.int32, (TNP, PS), 0)
        jn = lax.broadcasted_iota(jnp.int32, (TNP, PS), 1)
        visible = jnp.logical_and(jn <= tq, jn < TN)
        key_ref[n_pages] = jnp.where(visible, kn, jnp.int32(INT_MIN))
        for p in range(n_pages + 1, NPG):
            key_ref[p] = jnp.full((TNP, PS), INT_MIN, jnp.int32)

        def count(cand, strict):
            def body(p, a):
                kk = key_ref[p]
                return a + jnp.where((kk > cand) if strict else (kk >= cand), 1, 0)
            a = lax.fori_loop(0, n_pages + 1, body, jnp.zeros((TNP, PS), jnp.int32))
            return jnp.broadcast_to(jnp.sum(a, axis=1, keepdims=True), (TNP, PS))

        zero = jnp.zeros((TNP, PS), jnp.int32)
        thr0 = jnp.where(count(zero, False) >= n_sel, zero, jnp.int32(INT_MIN))

        def bit_body(i, thr):
            cand = thr | lax.shift_left(jnp.int32(1), 30 - i)
            return jnp.where(count(cand, False) >= n_sel, cand, thr)

        thr = lax.fori_loop(0, 31, bit_body, thr0)
        n_ties = (n_sel - count(thr, True)).astype(F32)
        has_thr = jnp.where(thr != jnp.int32(INT_MIN), 1.0, 0.0)

        r_k = lax.broadcasted_iota(jnp.int32, (PS, PS), 0)
        c_k = lax.broadcasted_iota(jnp.int32, (PS, PS), 1)
        u_incl = jnp.where(r_k <= c_k, 1.0, 0.0).astype(BF16)
        u_strict = jnp.where(r_k < c_k, 1.0, 0.0).astype(BF16)
        ones_k = jnp.ones((PS, PS), BF16)
        r_g = lax.broadcasted_iota(jnp.int32, (NPG, NPG), 0)
        c_g = lax.broadcasted_iota(jnp.int32, (NPG, NPG), 1)
        l_strict = jnp.where(c_g < r_g, 1.0, 0.0).astype(BF16)
        r_p = lax.broadcasted_iota(jnp.int32, (n_pages, n_pages), 0)
        c_p = lax.broadcasted_iota(jnp.int32, (n_pages, n_pages), 1)
        l_incl = jnp.where(c_p <= r_p, 1.0, 0.0).astype(BF16)
        rep = n_sel // LANES
        slot = lax.broadcasted_iota(jnp.int32, (n_pages, n_sel), 1).astype(F32)
        jcol = lax.broadcasted_iota(jnp.int32, (PS, n_sel), 0).astype(F32)
        wide = lambda a: jnp.concatenate([a] * rep, axis=1)
        pt_w = wide(ptcol_ref[0])

        for t in range(TN):
            kt = key_ref[:, t, :]
            thr_t = thr[t:t + 1, :]
            tie = jnp.where(kt == thr_t, has_thr[t:t + 1, :], 0.0)
            tie_b = tie.astype(BF16)
            row_ties = _dot(tie_b, ones_k)
            rank = _dot(l_strict, row_ties.astype(BF16)) + _dot(tie_b, u_incl)
            sel = jnp.where(kt > thr_t, 1.0, jnp.where(rank <= n_ties[t:t + 1, :], tie, 0.0))
            selnew_ref[0, t:t + 1, :] = sel[n_pages:n_pages + 1, :]
            sp = sel[0:n_pages]
            sp_b = sp.astype(BF16)
            within = _dot(sp_b, u_strict)
            cnt = _dot(sp_b, ones_k)
            end = _dot(l_incl, cnt.astype(BF16))
            off_w, end_w = wide(end - cnt), wide(end)
            a = jnp.where(off_w <= slot, jnp.where(slot < end_w, 1.0, 0.0), 0.0)
            valid = jnp.sum(a, axis=0, keepdims=True)
            local = slot[0:1] - jnp.sum(a * off_w, axis=0, keepdims=True)
            phys = jnp.sum(a * pt_w, axis=0, keepdims=True)
            wsel = _dot_tn((sp * (within + 1.0)).astype(BF16), a.astype(BF16))
            jrow = jnp.sum(jnp.where(wsel == local + 1.0, jcol, 0.0), axis=0, keepdims=True)
            row = jnp.where(valid > 0.5, phys * PS + jrow, 0.0)
            idx_ref[0, t:t + 1, :] = row.astype(jnp.int32)
            n_row = jnp.broadcast_to(jnp.sum(valid, axis=1, keepdims=True), (1, LANES))
            npast_ref[0, t:t + 1, :] = n_row.astype(jnp.int32)
        if TNP > TN:
            idx_ref[0, TN:TNP, :] = jnp.zeros((TNP - TN, n_sel), jnp.int32)
            npast_ref[0, TN:TNP, :] = jnp.zeros((TNP - TN, LANES), jnp.int32)
            selnew_ref[0, TN:TNP, :] = jnp.zeros((TNP - TN, LANES), F32)

    select()


def _dsa_s_select(page_table, iq_hq, wcol, iknew_t, pt_col, idx_cache_t, *, layer, n_sel, TN, TNP):
    DB, n_pages = page_table.shape
    assert n_sel % LANES == 0 and PAGE_SIZE == LANES
    PP = 16
    while n_pages % PP:
        PP //= 2
    R = iq_hq.shape[1]
    NPG = -(-(n_pages + 1) // SUBLANES) * SUBLANES
    bmap = lambda b, pt: (b, 0, 0)
    grid_spec = pltpu.PrefetchScalarGridSpec(
        num_scalar_prefetch=1, grid=(DB,),
        in_specs=[pl.BlockSpec((1, R, IDX_DIM), bmap),
                  pl.BlockSpec((1, R, LANES), bmap),
                  pl.BlockSpec((1, IDX_DIM, PAGE_SIZE), bmap),
                  pl.BlockSpec((1, n_pages, LANES), bmap),
                  pl.BlockSpec(memory_space=pl.ANY)],
        out_specs=[pl.BlockSpec((1, TNP, n_sel), bmap),
                   pl.BlockSpec((1, TNP, LANES), bmap),
                   pl.BlockSpec((1, TNP, LANES), bmap)],
        scratch_shapes=[pltpu.VMEM((NPG, TNP, PAGE_SIZE), jnp.int32),
                        pltpu.VMEM((2, n_pages, IDX_DIM, PAGE_SIZE), F32),
                        pltpu.SemaphoreType.DMA((2,))])
    return pl.pallas_call(
        functools.partial(_dsa_s_select_kernel, layer=layer, PP=PP, n_pages=n_pages, n_sel=n_sel, TN=TN, TNP=TNP),
        out_shape=[jax.ShapeDtypeStruct((DB, TNP, n_sel), jnp.int32),
                   jax.ShapeDtypeStruct((DB, TNP, LANES), jnp.int32),
                   jax.ShapeDtypeStruct((DB, TNP, LANES), F32)],
        grid_spec=grid_spec,
        compiler_params=_cparams(("arbitrary",)), name="dsa_sample_select",
    )(page_table, iq_hq, wcol, iknew_t, pt_col, idx_cache_t)


def _dsa_s_attn_kernel(idx_ref, idxn_ref, npast_ref, q_ref, knew_ref, vnew_ref, selnew_ref, ck_hbm, cv_hbm,
                       out_ref, kbuf, vbuf, sem, *, layer, TN, n_sel):
    NH, HD = ATTN_HEADS, ATTN_HD
    TNP = q_ref.shape[1]
    b = pl.program_id(0)
    nb = pl.num_programs(0)
    scale = HD ** -0.5
    page_shift = PAGE_SIZE.bit_length() - 1

    def issue(rows_ref, t, slot):
        def body(i, carry):
            row = rows_ref[0, t, i]
            pg = lax.shift_right_logical(row, page_shift)
            off = row & (PAGE_SIZE - 1)
            pltpu.make_async_copy(ck_hbm.at[layer, pg, off], kbuf.at[slot, i], sem.at[0, slot]).start()
            pltpu.make_async_copy(cv_hbm.at[layer, pg, off], vbuf.at[slot, i], sem.at[1, slot]).start()
            return carry
        lax.fori_loop(0, n_sel, body, 0, unroll=8)

    def wait(slot):
        pltpu.make_async_copy(kbuf.at[slot], kbuf.at[slot], sem.at[0, slot]).wait()
        pltpu.make_async_copy(vbuf.at[slot], vbuf.at[slot], sem.at[1, slot]).wait()

    def head_tile(ref, t):
        return jnp.concatenate([ref[0, t:t + 1, h * HD:(h + 1) * HD] for h in range(NH)], axis=0).astype(F32)

    def attend(t, slot):
        qt = head_tile(q_ref, t)
        kb = kbuf[slot]
        lg = jnp.sum(kb * qt[None], axis=-1, keepdims=True) * scale
        live = lax.broadcasted_iota(jnp.int32, (n_sel, NH, 1), 0) < npast_ref[0, t, 0]
        lg = jnp.where(live, lg, NEG_BIG)
        m = jnp.max(lg, axis=0)
        new = []
        for j in range(TN):
            keep = selnew_ref[0, t:t + 1, j:j + 1] > 0.5
            lj = jnp.sum(head_tile(knew_ref, j) * qt, axis=-1, keepdims=True) * scale
            lj = jnp.where(keep, lj, NEG_BIG)
            m = jnp.maximum(m, lj)
            new.append((keep, lj))
        p = jnp.where(live, jnp.exp(lg - m[None]), 0.0)
        den = jnp.sum(p, axis=0)
        o = jnp.sum(p * vbuf[slot], axis=0)
        for j, (keep, lj) in enumerate(new):
            pj = jnp.where(keep, jnp.exp(lj - m), 0.0)
            den = den + pj
            o = o + pj * head_tile(vnew_ref, j)
        o = o / den
        for h in range(NH):
            out_ref[0, t:t + 1, h * HD:(h + 1) * HD] = o[h:h + 1, :].astype(out_ref.dtype)

    @pl.when(b == 0)
    def _first():
        issue(idx_ref, 0, 0)

    if TNP > TN:
        out_ref[0, TN:TNP, :] = jnp.zeros((TNP - TN, NH * HD), out_ref.dtype)
    for t in range(TN):
        slot = t % 2
        if t + 1 < TN:
            issue(idx_ref, t + 1, 1 - slot)
        else:
            @pl.when(b + 1 < nb)
            def _next():
                issue(idxn_ref, 0, 1 - slot)
        wait(slot)
        attend(t, slot)


def _dsa_s_attn(sel_idx, n_past, sel_new, q, knew, vnew, cache_k, cache_v, *, layer, TN):
    DB, TNP, n_sel = sel_idx.shape
    assert TN % 2 == 0
    AW = ATTN_HEADS * ATTN_HD
    bmap = lambda b: (b, 0, 0)
    nxt = lambda b: (jnp.minimum(b + 1, DB - 1), 0, 0)
    smem = pltpu.SMEM
    return pl.pallas_call(
        functools.partial(_dsa_s_attn_kernel, layer=layer, TN=TN, n_sel=n_sel),
        out_shape=jax.ShapeDtypeStruct((DB, TNP, AW), BF16),
        grid=(DB,),
        in_specs=[pl.BlockSpec((1, TNP, n_sel), bmap, memory_space=smem),
                  pl.BlockSpec((1, TNP, n_sel), nxt, memory_space=smem),
                  pl.BlockSpec((1, TNP, LANES), bmap, memory_space=smem),
                  pl.BlockSpec((1, TNP, AW), bmap),
                  pl.BlockSpec((1, TNP, AW), bmap),
                  pl.BlockSpec((1, TNP, AW), bmap),
                  pl.BlockSpec((1, TNP, LANES), bmap),
                  pl.BlockSpec(memory_space=pl.ANY),
                  pl.BlockSpec(memory_space=pl.ANY)],
        out_specs=pl.BlockSpec((1, TNP, AW), bmap),
        scratch_shapes=[pltpu.VMEM((2, n_sel, ATTN_HEADS, ATTN_HD), F32),
                        pltpu.VMEM((2, n_sel, ATTN_HEADS, ATTN_HD), F32),
                        pltpu.SemaphoreType.DMA((2, 2))],
        compiler_params=_cparams(("arbitrary",)), name="dsa_sample_attn",
    )(sel_idx, sel_idx, n_past, q, knew, vnew, sel_new, cache_k, cache_v)


def _mm_res_ln_kernel(*refs, alpha, nk, n_a, n_out):
    a_refs = refs[:n_a]
    w_ref, res_ref, g_ref, b_ref = refs[n_a:n_a + 4]
    out_refs = refs[n_a + 4:n_a + 4 + n_out]
    acc_ref = refs[n_a + 4 + n_out]
    k = pl.program_id(1)

    def finish(total):
        y = alpha * res_ref[...] + total
        mu = jnp.mean(y, axis=1, keepdims=True)
        var = jnp.mean(jnp.square(y - mu), axis=1, keepdims=True)
        out = (y - mu) * lax.rsqrt(var + LN_EPS) * g_ref[...] + b_ref[...]
        for o in out_refs:
            o[...] = out.astype(o.dtype)

    def step(first, last):
        if n_a == 1:
            part = _dot(a_refs[0][...], w_ref[...])
        else:
            kw = a_refs[0].shape[1]
            part = sum(_dot(a[...], w_ref[i * kw:(i + 1) * kw, :]) for i, a in enumerate(a_refs))
        if first and last:
            finish(part)
        elif first:
            acc_ref[...] = part
        elif last:
            finish(acc_ref[...] + part)
        else:
            acc_ref[...] += part

    if nk == 1:
        step(True, True)
    else:
        pl.when(k == 0)(functools.partial(step, True, False))
        pl.when(jnp.logical_and(k > 0, k < nk - 1))(functools.partial(step, False, False))
        pl.when(k == nk - 1)(functools.partial(step, False, True))


def _mm_res_ln(a_list, w, res, gamma, beta, *, alpha, tk=None, out_dtypes=(F32, BF16), name="mm_res_ln"):
    M = a_list[0].shape[0]
    N = w.shape[1]
    K = w.shape[0]
    tm = min(512, M)
    n_a = len(a_list)
    if n_a > 1:
        kw = a_list[0].shape[1]
        assert all(a.shape == (M, kw) for a in a_list) and kw * n_a == K
        tk = K
        a_specs = [pl.BlockSpec((tm, kw), lambda m, k: (m, 0)) for _ in a_list]
    else:
        tk = K if tk is None else tk
        assert a_list[0].shape == (M, K) and K % tk == 0
        a_specs = [pl.BlockSpec((tm, tk), lambda m, k: (m, k))]
    nk = K // tk
    assert M % tm == 0
    row = lambda m, k: (m, 0)
    return pl.pallas_call(
        functools.partial(_mm_res_ln_kernel, alpha=alpha, nk=nk, n_a=n_a, n_out=len(out_dtypes)),
        out_shape=[jax.ShapeDtypeStruct((M, N), dt) for dt in out_dtypes],
        grid=(M // tm, nk),
        in_specs=a_specs + [pl.BlockSpec((tk, N), lambda m, k: (k, 0)),
                            pl.BlockSpec((tm, N), row),
                            pl.BlockSpec((1, N), lambda m, k: (0, 0)),
                            pl.BlockSpec((1, N), lambda m, k: (0, 0))],
        out_specs=[pl.BlockSpec((tm, N), row) for _ in out_dtypes],
        scratch_shapes=[pltpu.VMEM((tm, N) if nk > 1 else (SUBLANES, LANES), F32)],
        compiler_params=_cparams(("parallel", "arbitrary")), name=name,
    )(*a_list, w, res, gamma, beta)


def _xattn_kernel(q_ref, mk_ref, mv_ref, o_ref):
    D = q_ref.shape[-1]
    hd = D // MEM_HEADS
    scale = hd ** -0.5
    per_head = len(mk_ref.shape) == 4
    for h in range(MEM_HEADS):
        q = q_ref[0, :, h * hd:(h + 1) * hd]
        if per_head:
            kh = mk_ref[0, :, h, :].astype(BF16)
            vh = mv_ref[0, :, h, :].astype(BF16)
        else:
            kh = mk_ref[0, :, h * hd:(h + 1) * hd].astype(BF16)
            vh = mv_ref[0, :, h * hd:(h + 1) * hd].astype(BF16)
        lg = _dot_nt(q, kh) * scale
        e = jnp.exp(lg - jnp.max(lg, axis=1, keepdims=True))
        p = e / jnp.sum(e, axis=1, keepdims=True)
        o_ref[0, :, h * hd:(h + 1) * hd] = _dot(p.astype(BF16), vh).astype(o_ref.dtype)


def _xattn(q, mem_k, mem_v):
    nb, Tq, D = q.shape
    S = mem_k.shape[1]
    tq = min(512, Tq)
    assert Tq % tq == 0
    mem_spec = pl.BlockSpec((1,) + mem_k.shape[1:], lambda b, i: (b,) + (0,) * (mem_k.ndim - 1))
    return pl.pallas_call(
        _xattn_kernel,
        out_shape=jax.ShapeDtypeStruct((nb, Tq, D), BF16),
        grid=(nb, Tq // tq),
        in_specs=[pl.BlockSpec((1, tq, D), lambda b, i: (b, i, 0)), mem_spec, mem_spec],
        out_specs=pl.BlockSpec((1, tq, D), lambda b, i: (b, i, 0)),
        compiler_params=_cparams(("parallel", "arbitrary")), name="xattn",
    )(q, mem_k, mem_v)


def _ffn_act_kernel(x_ref, wg_ref, wu_ref, o_ref):
    x = x_ref[...]
    gte = _dot(x, wg_ref[...])
    up = _dot(x, wu_ref[...])
    o_ref[...] = (gte * jax.nn.sigmoid(gte) * up).astype(o_ref.dtype)


def _ffn_tile(n):
    for t in (512, 256, 128):
        if n % t == 0:
            return t
    return n


def _ffn_act(x, wg, wu):
    M, K = x.shape
    N = wg.shape[1]
    tm = min(1024, M)
    tn = _ffn_tile(N)
    assert M % tm == 0
    return pl.pallas_call(
        _ffn_act_kernel,
        out_shape=jax.ShapeDtypeStruct((M, N), BF16),
        grid=(N // tn, M // tm),
        in_specs=[pl.BlockSpec((tm, K), lambda n, m: (m, 0)),
                  pl.BlockSpec((K, tn), lambda n, m: (0, n)),
                  pl.BlockSpec((K, tn), lambda n, m: (0, n))],
        out_specs=pl.BlockSpec((tm, tn), lambda n, m: (m, n)),
        compiler_params=_cparams(("parallel", "parallel")), name="ffn_act",
    )(x, wg, wu)


def _in_weights(w_in):
    H, DK, DV = MLSTM_HEADS, MLSTM_QK, MLSTM_V
    aw = ATTN_HEADS * ATTN_HD
    iw = IDX_HEADS * IDX_DIM
    o = 0
    w_qkv = w_in[:, o:o + 2 * H * DK + H * DV]; o += 2 * H * DK + H * DV
    w_mo = w_in[:, o:o + H * DV]; o += H * DV
    w_ig = w_in[:, o:o + H]; o += H
    w_fg = w_in[:, o:o + H]; o += H
    w_aq = w_in[:, o:o + aw]; o += aw
    w_ak = w_in[:, o:o + aw]; o += aw
    w_av = w_in[:, o:o + aw]; o += aw
    w_iq = w_in[:, o:o + iw]; o += iw
    w_ik = w_in[:, o:o + IDX_DIM]; o += IDX_DIM
    w_iw = w_in[:, o:o + IDX_HEADS]; o += IDX_HEADS
    assert o == w_in.shape[1]
    pad = jnp.zeros((w_in.shape[0], LANES - IDX_DIM - 2 * H - IDX_HEADS), w_in.dtype)
    w_small = jnp.concatenate([w_ik, w_ig, w_fg, w_iw, pad], axis=1)
    return tuple(w.astype(BF16) for w in (w_qkv, w_mo, w_aq, w_ak, w_av, w_iq, w_small))


def _mixer_inputs(xb, w, bias_small, pos, tag):
    w_qkv, w_mo, w_aq, w_ak, w_av, w_iq, w_small = w
    cos_a, sin_a = _rope_tables(pos, ATTN_HD)
    cos_i, sin_i = _rope_tables(pos, IDX_DIM)
    cos_s, sin_s = _small_tables(pos)
    (qkv,) = _proj(xb, w_qkv, [BF16], name=tag + "_proj_qkv")
    (mo,) = _proj(xb, w_mo, [F32], name=tag + "_proj_mo")
    (aq,) = _proj(xb, w_aq, [BF16], rope=1, cos=cos_a, sin=sin_a, name=tag + "_proj_aq")
    ak, akb = _proj(xb, w_ak, [F32, BF16], rope=1, cos=cos_a, sin=sin_a, name=tag + "_proj_ak")
    av, avb = _proj(xb, w_av, [F32, BF16], name=tag + "_proj_av")
    (iq,) = _proj(xb, w_iq, [BF16], rope=2, cos=cos_i, sin=sin_i, name=tag + "_proj_iq")
    (small,) = _proj(xb, w_small, [F32], rope=2, cos=cos_s, sin=sin_s, bias=bias_small,
                     name=tag + "_proj_small")
    return qkv, mo, aq, ak, akb, av, avb, iq, small


def _block_tail(x2d, heads, mem_k, mem_v, nb, w_out, ln1, w_cq, w_co, ln2, w_gate, w_up, w_down, ln3,
                alpha, tag):
    M, D = x2d.shape
    x1, x1b = _mm_res_ln(heads, w_out, x2d, ln1[0], ln1[1], alpha=alpha, name=tag + "_out_ln1")
    (cq,) = _proj(x1b, w_cq, [BF16], name=tag + "_proj_cq")
    ctx = _xattn(cq.reshape(nb, M // nb, D), mem_k, mem_v).reshape(M, D)
    x2, x2b = _mm_res_ln([ctx], w_co, x1, ln2[0], ln2[1], alpha=alpha, name=tag + "_co_ln2")
    act = _ffn_act(x2b, w_gate, w_up)
    F = act.shape[1]
    tk = F
    for cand in (1408, 1024, 512):
        if F % cand == 0 and F > cand:
            tk = cand
            break
    (x3,) = _mm_res_ln([act], w_down, x2, ln3[0], ln3[1], alpha=alpha, tk=tk, out_dtypes=(F32,),
                       name=tag + "_down_ln3")
    return x3


def kernel(x_prompt, x_sample, cache_k, cache_v, cache_idx_k, state_C, state_n, state_m, cache_mem_k, cache_mem_v, page_table, mem_prompt, w_in, b_igate, b_fgate, mlstm_norm_g, w_out, ln1_g, ln1_b, w_cq, w_ck, w_cv, w_co, ln2_g, ln2_b, w_gate, w_up, w_down, ln3_g, ln3_b):
    B, T, D = x_prompt.shape
    DB, TN, _ = x_sample.shape
    depth = w_in.shape[0]
    n_pages = page_table.shape[1]
    past = n_pages * PAGE_SIZE
    n_mem = mem_prompt.shape[1]
    H, DK, DV = MLSTM_HEADS, MLSTM_QK, MLSTM_V
    AW = ATTN_HEADS * ATTN_HD
    MIX = H * DV + AW
    TNP = -(-TN // SUBLANES) * SUBLANES
    assert TNP == SUBLANES
    alpha = (2 * depth) ** 0.25
    n_sel_p = min(TOPK_MAX, T // 4)
    n_sel_s = min(TOPK_MAX, (past + TN) // 4)
    L = MLSTM_CHUNK if T % MLSTM_CHUNK == 0 else T
    page_table = page_table.astype(jnp.int32)
    idx_cache_t = jnp.swapaxes(cache_idx_k, 2, 3)

    pos_p = jnp.arange(T)
    pos_s = past + (jnp.arange(DB * TNP) % TNP)

    xp = x_prompt.reshape(B * T, D)
    xs = jnp.pad(x_sample, ((0, 0), (0, TNP - TN), (0, 0))).reshape(DB * TNP, D)
    new = [[] for _ in range(14)]
    row2 = lambda a: a.reshape(1, -1).astype(F32)

    for l in range(depth):
        w_l = _in_weights(w_in[l])
        zpad = jnp.zeros((LANES - SM_IW,), F32)
        bias_small = jnp.concatenate([jnp.zeros((SM_IG,), F32), b_igate[l].astype(F32),
                                      b_fgate[l].astype(F32), zpad]).reshape(1, LANES)
        norm_g = row2(mlstm_norm_g[l])
        w_out_b, w_cq_b, w_co_b = (w.astype(BF16) for w in (w_out[l], w_cq[l], w_co[l]))
        w_gate_b, w_up_b, w_down_b = (w.astype(BF16) for w in (w_gate[l], w_up[l], w_down[l]))
        ln1 = (row2(ln1_g[l]), row2(ln1_b[l]))
        ln2 = (row2(ln2_g[l]), row2(ln2_b[l]))
        ln3 = (row2(ln3_g[l]), row2(ln3_b[l]))
        tail_w = (w_out_b, ln1, w_cq_b, w_co_b, ln2, w_gate_b, w_up_b, w_down_b, ln3)

        qkv, mo, aq, ak, akb, av, avb, iq, small = _mixer_inputs(xp.astype(BF16), w_l, bias_small, pos_p, "p")
        hm, pC, pn, pm = _mlstm(
            qkv, small, mo, norm_g, jnp.zeros((B, H, DV, DK), F32), jnp.zeros((B, H, DK), F32),
            jnp.zeros((B, H), F32), nb=B, T=T, L=L, valid=L)
        ha = _dsa_prompt(iq, small, aq, akb, avb, nb=B, T=T, n_sel=n_sel_p)
        memb = mem_prompt.reshape(B * n_mem, D).astype(BF16)
        (mem_k,) = _proj(memb, w_ck[l].astype(BF16), [F32], name="p_proj_ck")
        (mem_v,) = _proj(memb, w_cv[l].astype(BF16), [F32], name="p_proj_cv")
        mem_k = mem_k.reshape(B, n_mem, D)
        mem_v = mem_v.reshape(B, n_mem, D)
        xp = _block_tail(xp, [hm, ha], mem_k, mem_v, B, *tail_w, alpha, "p")

        sqkv, smo, saq, sak, sakb, sav, savb, siq, ssmall = _mixer_inputs(
            xs.astype(BF16), w_l, bias_small, pos_s, "s")
        shm, sC, sn, sm = _mlstm(sqkv, ssmall, smo, norm_g, state_C[l].astype(F32), state_n[l].astype(F32),
                                 state_m[l].astype(F32), nb=DB, T=TNP, L=TNP, valid=TN)
        iq_hq = siq.reshape(DB, TNP, IDX_HEADS, IDX_DIM).transpose(0, 2, 1, 3).reshape(DB, IDX_HEADS * TNP, IDX_DIM)
        w_hq = ssmall[:, SM_IW:SM_IW + IDX_HEADS].reshape(DB, TNP, IDX_HEADS).transpose(0, 2, 1)
        w_hq = w_hq.reshape(DB, IDX_HEADS * TNP, 1) * (IDX_HEADS * IDX_DIM) ** -0.5
        wcol = jnp.broadcast_to(w_hq, (DB, IDX_HEADS * TNP, LANES))
        ik_new_t = jnp.pad(ssmall[:, SM_IK:SM_IK + IDX_DIM].reshape(DB, TNP, IDX_DIM),
                           ((0, 0), (0, PAGE_SIZE - TNP), (0, 0))).swapaxes(1, 2)
        pt_col = jnp.broadcast_to(page_table.astype(F32)[:, :, None], (DB, n_pages, LANES))
        sel_idx, n_past, sel_new = _dsa_s_select(page_table, iq_hq, wcol, ik_new_t, pt_col, idx_cache_t,
                                                 layer=l, n_sel=n_sel_s, TN=TN, TNP=TNP)
        sha = _dsa_s_attn(sel_idx, n_past, sel_new, saq.reshape(DB, TNP, AW), sakb.reshape(DB, TNP, AW),
                          savb.reshape(DB, TNP, AW), cache_k, cache_v, layer=l, TN=TN).reshape(DB * TNP, AW)
        xs = _block_tail(xs, [shm, sha], cache_mem_k[l], cache_mem_v[l], DB, *tail_w, alpha, "s")

        def s_rows(a, shape):
            return a.reshape(DB, TNP, -1)[:, :TN].reshape(shape)

        vals = (ak.reshape(B, T, ATTN_HEADS, ATTN_HD), av.reshape(B, T, ATTN_HEADS, ATTN_HD),
                small[:, SM_IK:SM_IK + IDX_DIM].reshape(B, T, IDX_DIM), pC, pn, pm,
                mem_k.reshape(B, n_mem, MEM_HEADS, D // MEM_HEADS), mem_v.reshape(B, n_mem, MEM_HEADS, D // MEM_HEADS),
                s_rows(sak, (DB, TN, ATTN_HEADS, ATTN_HD)), s_rows(sav, (DB, TN, ATTN_HEADS, ATTN_HD)),
                s_rows(ssmall[:, SM_IK:SM_IK + IDX_DIM], (DB, TN, IDX_DIM)), sC, sn, sm)
        for slot, val in zip(new, vals):
            slot.append(val)

    outs = [jnp.stack(slot) for slot in new]
    y_prompt = xp.reshape(B, T, D)
    y_sample = xs.reshape(DB, TNP, D)[:, :TN]
    return (y_prompt, y_sample, *outs)
```
